```python
import math
import jax, jax.numpy as jnp
from jax import lax
import numpy as np

D_MODEL = 1024
BATCH = 1
SEQ = 16384
DEPTH = 2
DEC_BATCH = 8
DEC_SEQ = 16
PAST_LEN = 4096

CHUNK = 64
Q_BLOCK = 128
EPS = 1e-6
MLA_HEADS = 8
Q_LORA = 384
KV_LORA = 256
NOPE_DIM = 128
ROPE_DIM = 64
V_DIM = 128
MLA_WIDTH = MLA_HEADS * V_DIM
ROPE_THETA = 10000.0
ATTN_SCALE = (NOPE_DIM + ROPE_DIM) ** -0.5
SSD_HEADS = 16
SSD_HEAD_DIM = 64
SSD_WIDTH = SSD_HEADS * SSD_HEAD_DIM
SSD_GROUPS = 2
SSD_STATE = 128
CONV_W = 4
CONV_DIM = SSD_WIDTH + 2 * SSD_GROUPS * SSD_STATE
DT_MIN = 0.001
DT_MAX = 0.1
MIX_WIDTH = MLA_WIDTH + SSD_WIDTH
IN_SPLITS = (Q_LORA, KV_LORA, ROPE_DIM, MLA_WIDTH, SSD_WIDTH, CONV_DIM, SSD_HEADS)
IN_WIDTH = 4304

kernel_name = "hybrid_mla_ssd_stream_step"


def rmsnorm(x, w):
    xf = x.astype(jnp.float32)
    y = xf * lax.rsqrt(jnp.mean(xf * xf, axis=-1, keepdims=True) + EPS)
    return (y * w.astype(jnp.float32)).astype(x.dtype)


def rope(x, pos):
    half = ROPE_DIM // 2
    inv = 1.0 / (ROPE_THETA ** (jnp.arange(half, dtype=jnp.float32) / half))
    ang = pos.astype(jnp.float32)[:, None] * inv[None, :]
    shape = (ang.shape[0],) + (1,) * (x.ndim - 3) + (half,)
    cos = jnp.cos(ang).reshape(shape)
    sin = jnp.sin(ang).reshape(shape)
    xf = x.astype(jnp.float32)
    x1, x2 = xf[..., :half], xf[..., half:]
    return jnp.concatenate([x1 * cos - x2 * sin, x2 * cos + x1 * sin], axis=-1).astype(x.dtype)


def mla_attend(q_lat, q_pe, ckv, kpe, q_pos, k_pos):
    s = (jnp.einsum('blhc,bsc->bhls', q_lat, ckv)
         + jnp.einsum('blhr,bsr->bhls', q_pe, kpe)).astype(jnp.float32) * ATTN_SCALE
    allowed = (k_pos[None, :] // CHUNK) <= (q_pos[:, None] // CHUNK)
    s = jnp.where(allowed, s, -jnp.inf)
    p = jax.nn.softmax(s, axis=-1).astype(ckv.dtype)
    return jnp.einsum('bhls,bsc->blhc', p, ckv)


def causal_conv(xbc, conv_state, w, b):
    L = xbc.shape[1]
    full = jnp.concatenate([conv_state, xbc], axis=1)
    y = b
    for k in range(CONV_W):
        y = y + full[:, k:k + L] * w[k]
    return jax.nn.silu(y), full[:, -(CONV_W - 1):]


def segsum(a):
    T = a.shape[-1]
    cs = jnp.cumsum(a, axis=-1)
    diff = cs[..., :, None] - cs[..., None, :]
    mask = jnp.tril(jnp.ones((T, T), dtype=bool))
    return jnp.where(mask, diff, -jnp.inf)


def ssd_scan(x, dt, A, B, C, h0):
    b, L, H, P = x.shape
    G, N, HG = SSD_GROUPS, SSD_STATE, SSD_HEADS // SSD_GROUPS
    cl = min(CHUNK, L)
    nc = L // cl
    xd = (x * dt[..., None]).reshape(b, nc, cl, G, HG, P)
    a = jnp.moveaxis((dt * A).reshape(b, nc, cl, G, HG), 2, -1)
    Bc = B.reshape(b, nc, cl, G, N)
    Cc = C.reshape(b, nc, cl, G, N)
    a_cum = jnp.cumsum(a, axis=-1)
    decay = jnp.exp(segsum(a))
    cb = jnp.einsum('bclgn,bcsgn->bcgls', Cc, Bc)
    y_diag = jnp.einsum('bcghls,bcsghp->bclghp', cb[:, :, :, None] * decay, xd)
    decay_states = jnp.exp(a_cum[..., -1:] - a_cum)
    states = jnp.einsum('bcsgn,bcghs,bcsghp->bcghpn', Bc, decay_states, xd)
    chunk_decay = jnp.exp(a_cum[..., -1])

    def step(h, inp):
        st, dec = inp
        return h * dec[..., None, None] + st, h

    h_last, h_prev = lax.scan(step, h0.reshape(b, G, HG, P, N),
                              (jnp.moveaxis(states, 1, 0), jnp.moveaxis(chunk_decay, 1, 0)))
    h_prev = jnp.moveaxis(h_prev, 0, 1)
    y_off = jnp.einsum('bclgn,bcghpn,bcghl->bclghp', Cc, h_prev, jnp.exp(a_cum))
    y = (y_diag + y_off).reshape(b, L, H, P)
    return y, h_last.reshape(b, H, P, N)


def layer(x, ckv_past, kpe_past, conv_state, ssm_state,
          norm_w, w_in, q_norm_w, w_uq, kv_norm_w, w_uk, w_uv,
          conv_w, conv_b, dt_bias, a_log, d_skip, ssd_norm_w, w_out):
    b, L, _ = x.shape
    past = ckv_past.shape[1]
    q_pos = past + jnp.arange(L)
    k_pos = jnp.arange(past + L)
    h = rmsnorm(x, norm_w)
    proj = h @ w_in
    offs = [int(o) for o in np.cumsum(IN_SPLITS)[:-1]]
    q_c, kv_c, k_pe, g_mla, z, xbc, dt_raw = jnp.split(proj, offs, axis=-1)

    q = (rmsnorm(q_c, q_norm_w) @ w_uq).reshape(b, L, MLA_HEADS, NOPE_DIM + ROPE_DIM)
    q_nope, q_pe = q[..., :NOPE_DIM], rope(q[..., NOPE_DIM:], q_pos)
    c_kv = rmsnorm(kv_c, kv_norm_w)
    k_pe = rope(k_pe, q_pos)
    q_lat = jnp.einsum('blhd,chd->blhc', q_nope, w_uk)
    ckv_all = jnp.concatenate([ckv_past, c_kv], axis=1)
    kpe_all = jnp.concatenate([kpe_past, k_pe], axis=1)
    blk = min(Q_BLOCK, L)
    nq = L // blk
    qb = jnp.swapaxes(q_lat.reshape(b, nq, blk, MLA_HEADS, KV_LORA), 0, 1)
    pb = jnp.swapaxes(q_pe.reshape(b, nq, blk, MLA_HEADS, ROPE_DIM), 0, 1)
    posb = q_pos.reshape(nq, blk)
    o_lat = lax.map(lambda t: mla_attend(t[0], t[1], ckv_all, kpe_all, t[2], k_pos), (qb, pb, posb))
    o_lat = jnp.swapaxes(o_lat, 0, 1).reshape(b, L, MLA_HEADS, KV_LORA)
    o = jnp.einsum('blhc,chv->blhv', o_lat, w_uv).reshape(b, L, MLA_WIDTH)
    mla_out = o * jax.nn.silu(g_mla)

    xbc_act, conv_new = causal_conv(xbc, conv_state, conv_w, conv_b)
    xs, Bm, Cm = jnp.split(xbc_act, [SSD_WIDTH, SSD_WIDTH + SSD_GROUPS * SSD_STATE], axis=-1)
    xs = xs.astype(jnp.float32).reshape(b, L, SSD_HEADS, SSD_HEAD_DIM)
    dt = jax.nn.softplus(dt_raw.astype(jnp.float32) + dt_bias.astype(jnp.float32))
    A = -jnp.exp(a_log.astype(jnp.float32))
    y, h_new = ssd_scan(xs, dt, A,
                        Bm.astype(jnp.float32).reshape(b, L, SSD_GROUPS, SSD_STATE),
                        Cm.astype(jnp.float32).reshape(b, L, SSD_GROUPS, SSD_STATE),
                        ssm_state.astype(jnp.float32))
    y = y + xs * d_skip.astype(jnp.float32)[:, None]
    y = y.reshape(b, L, SSD_WIDTH) * jax.nn.silu(z.astype(jnp.float32))
    y = rmsnorm(y.reshape(b, L, SSD_GROUPS, SSD_WIDTH // SSD_GROUPS),
                ssd_norm_w.reshape(SSD_GROUPS, SSD_WIDTH // SSD_GROUPS))
    ssd_out = y.reshape(b, L, SSD_WIDTH).astype(x.dtype)

    out = jnp.concatenate([mla_out, ssd_out], axis=-1) @ w_out
    return x + out, c_kv, k_pe, conv_new, h_new.astype(x.dtype)


def setup_inputs(seed: int = 0) -> dict:
    key = jax.random.key(seed)
    ks = jax.random.split(key, 24)
    f32 = jnp.float32

    def nrm(k, shape, scale):
        return jax.random.normal(k, shape, f32) * scale

    u = jax.random.uniform(ks[15], (DEPTH, SSD_HEADS), f32)
    dt0 = jnp.exp(u * (math.log(DT_MAX) - math.log(DT_MIN)) + math.log(DT_MIN))
    return {
        "x_prompt": nrm(ks[0], (BATCH, SEQ, D_MODEL), 1.0),
        "x_sample": nrm(ks[1], (DEC_BATCH, DEC_SEQ, D_MODEL), 1.0),
        "cache_ckv": nrm(ks[2], (DEPTH, DEC_BATCH, PAST_LEN, KV_LORA), 1.0),
        "cache_kpe": nrm(ks[3], (DEPTH, DEC_BATCH, PAST_LEN, ROPE_DIM), 1.0),
        "state_conv": nrm(ks[4], (DEPTH, DEC_BATCH, CONV_W - 1, CONV_DIM), 1.0),
        "state_ssm": nrm(ks[5], (DEPTH, DEC_BATCH, SSD_HEADS, SSD_HEAD_DIM, SSD_STATE), 0.1),
        "norm_w": 1.0 + nrm(ks[6], (DEPTH, D_MODEL), 0.01),
        "w_in": nrm(ks[7], (DEPTH, D_MODEL, IN_WIDTH), D_MODEL ** -0.5),
        "q_norm_w": 1.0 + nrm(ks[8], (DEPTH, Q_LORA), 0.01),
        "w_uq": nrm(ks[9], (DEPTH, Q_LORA, MLA_HEADS * (NOPE_DIM + ROPE_DIM)), Q_LORA ** -0.5),
        "kv_norm_w": 1.0 + nrm(ks[10], (DEPTH, KV_LORA), 0.01),
        "w_uk": nrm(ks[11], (DEPTH, KV_LORA, MLA_HEADS, NOPE_DIM), KV_LORA ** -0.5),
        "w_uv": nrm(ks[12], (DEPTH, KV_LORA, MLA_HEADS, V_DIM), KV_LORA ** -0.5),
        "conv_w": nrm(ks[13], (DEPTH, CONV_W, CONV_DIM), CONV_W ** -0.5),
        "conv_b": nrm(ks[14], (DEPTH, CONV_DIM), 0.01),
        "dt_bias": dt0 + jnp.log(-jnp.expm1(-dt0)),
        "a_log": jnp.log(jax.random.uniform(ks[16], (DEPTH, SSD_HEADS), f32, minval=1.0, maxval=16.0)),
        "d_skip": 1.0 + nrm(ks[17], (DEPTH, SSD_HEADS), 0.01),
        "ssd_norm_w": 1.0 + nrm(ks[18], (DEPTH, SSD_WIDTH), 0.01),
        "w_out": nrm(ks[19], (DEPTH, MIX_WIDTH, D_MODEL), MIX_WIDTH ** -0.5),
        "final_norm_w": 1.0 + nrm(ks[20], (D_MODEL,), 0.01),
    }


def reference(x_prompt, x_sample, cache_ckv, cache_kpe, state_conv, state_ssm,
              norm_w, w_in, q_norm_w, w_uq, kv_norm_w, w_uk, w_uv,
              conv_w, conv_b, dt_bias, a_log, d_skip, ssd_norm_w, w_out, final_norm_w):
    dt_ = x_prompt.dtype
    bp = x_prompt.shape[0]
    yp, ys = x_prompt, x_sample
    ckv_p, kpe_p, conv_p, ssm_p = [], [], [], []
    ckv_s, kpe_s, conv_s, ssm_s = [], [], [], []
    for i in range(DEPTH):
        params = (norm_w[i], w_in[i], q_norm_w[i], w_uq[i], kv_norm_w[i], w_uk[i], w_uv[i],
                  conv_w[i], conv_b[i], dt_bias[i], a_log[i], d_skip[i], ssd_norm_w[i], w_out[i])
        yp, a1, a2, a3, a4 = layer(
            yp,
            jnp.zeros((bp, 0, KV_LORA), dt_),
            jnp.zeros((bp, 0, ROPE_DIM), dt_),
            jnp.zeros((bp, CONV_W - 1, CONV_DIM), dt_),
            jnp.zeros((bp, SSD_HEADS, SSD_HEAD_DIM, SSD_STATE), dt_),
            *params)
        ckv_p.append(a1); kpe_p.append(a2); conv_p.append(a3); ssm_p.append(a4)
        ys, b1, b2, b3, b4 = layer(ys, cache_ckv[i], cache_kpe[i], state_conv[i], state_ssm[i], *params)
        ckv_s.append(b1); kpe_s.append(b2); conv_s.append(b3); ssm_s.append(b4)
    y_prompt = rmsnorm(yp, final_norm_w)
    y_sample = rmsnorm(ys, final_norm_w)
    return (y_prompt, y_sample,
            jnp.stack(ckv_p), jnp.stack(kpe_p), jnp.stack(conv_p), jnp.stack(ssm_p),
            jnp.stack(ckv_s), jnp.stack(kpe_s), jnp.stack(conv_s), jnp.stack(ssm_s))
```

```python
import functools

import numpy as np
import jax
import jax.numpy as jnp
from jax import lax
from jax.experimental import pallas as pl
from jax.experimental.pallas import tpu as pltpu

F32 = jnp.float32
BF16 = jnp.bfloat16

EPS = 1e-6
CHUNK = 64
MLA_HEADS = 8
Q_LORA = 384
KV_LORA = 256
NOPE_DIM = 128
ROPE_DIM = 64
V_DIM = 128
ROPE_THETA = 10000.0
ATTN_SCALE = (NOPE_DIM + ROPE_DIM) ** -0.5
SSD_HEADS = 16
SSD_HEAD_DIM = 64
SSD_GROUPS = 2
SSD_STATE = 128
CONV_W = 4
SSD_WIDTH = SSD_HEADS * SSD_HEAD_DIM
GROUP_WIDTH = SSD_WIDTH // SSD_GROUPS
HEADS_PER_GROUP = SSD_HEADS // SSD_GROUPS
CONV_DIM = SSD_WIDTH + 2 * SSD_GROUPS * SSD_STATE
MLA_WIDTH = MLA_HEADS * V_DIM
QCAT = KV_LORA + 2 * ROPE_DIM

LANES = 128
SUBLANES = 8
VMEM_LIMIT = 56 * 1024 * 1024


def _dot(a, b):
    return jnp.dot(a, b, preferred_element_type=F32)


def _dot_exact(a, b):
    return jnp.dot(a, b, preferred_element_type=F32, precision=lax.Precision.HIGHEST)


def _rms(x, w):
    return x * lax.rsqrt(jnp.mean(x * x, axis=-1, keepdims=True) + EPS) * w


def _silu(x):
    return x * (1.0 / (1.0 + jnp.exp(-x)))


def _const_spec(shape):
    nd = len(shape)
    return pl.BlockSpec(shape, lambda *_: (0,) * nd)


def _pre_kernel(x_ref, nw_ref, wq_ref, wkv_ref, wg_ref, wz_ref, wxbc_ref, wsm_ref,
                qnw_ref, wuqn_ref, wuqp_ref, wuqps_ref, kvnw_ref, wuk_ref, cos_ref, sin_ref,
                qcat_ref, ckv_ref, kpe_ref, kcat_ref, g_ref, z_ref, xbc_ref, dt_ref):
    x = x_ref[...]
    h = _rms(x, nw_ref[...]).astype(BF16)
    g_ref[...] = _dot(h, wg_ref[...])
    z_ref[...] = _dot(h, wz_ref[...])
    xbc_ref[...] = _dot(h, wxbc_ref[...])
    cos2 = cos_ref[...]
    sin2 = sin_ref[...]
    sm = _dot(h, wsm_ref[...])
    kpe2 = sm[:, 0:LANES] * cos2 + sm[:, LANES:2 * LANES] * sin2
    dt_ref[...] = sm[:, 2 * LANES:3 * LANES]
    kpe_ref[...] = kpe2[:, :ROPE_DIM]
    ckv = _rms(_dot(h, wkv_ref[...]), kvnw_ref[...])
    ckv_ref[...] = ckv
    kcat_ref[...] = jnp.concatenate([ckv, kpe2], axis=-1).astype(BF16)

    qn = _rms(_dot(h, wq_ref[...]), qnw_ref[...]).astype(BF16)
    qnope = _dot(qn, wuqn_ref[...])
    pairs = MLA_HEADS // 2
    cos_all = jnp.concatenate([cos2] * pairs, axis=-1)
    sin_all = jnp.concatenate([sin2] * pairs, axis=-1)
    qpe = _dot(qn, wuqp_ref[...]) * cos_all + _dot(qn, wuqps_ref[...]) * sin_all
    lane = lax.broadcasted_iota(jnp.int32, (x.shape[0], LANES), 1)
    for hd in range(MLA_HEADS):
        qlat = _dot(qnope[:, hd * NOPE_DIM:(hd + 1) * NOPE_DIM].astype(BF16), wuk_ref[hd])
        pair = qpe[:, (hd // 2) * LANES:(hd // 2 + 1) * LANES]
        keep = (lane < ROPE_DIM) if hd % 2 == 0 else (lane >= ROPE_DIM)
        qcat_ref[hd] = jnp.concatenate([qlat, jnp.where(keep, pair, 0.0)], axis=-1).astype(BF16)


def _pre_call(x, w, cos2, sin2, tr):
    rows, d = x.shape
    row = lambda n: pl.BlockSpec((tr, n), lambda i: (i, 0))
    weights = (w["norm_w"], w["w_q"], w["w_kv"], w["w_g"], w["w_z"], w["w_xbc"], w["w_sm"],
               w["q_norm_w"], w["w_uq_nope"], w["w_uq_pe"], w["w_uq_pe_sw"], w["kv_norm_w"], w["w_uk_t"])
    in_specs = [row(d)] + [_const_spec(a.shape) for a in weights] + [row(LANES), row(LANES)]
    out_shape = (
        jax.ShapeDtypeStruct((MLA_HEADS, rows, QCAT), BF16),
        jax.ShapeDtypeStruct((rows, KV_LORA), F32),
        jax.ShapeDtypeStruct((rows, ROPE_DIM), F32),
        jax.ShapeDtypeStruct((rows, QCAT), BF16),
        jax.ShapeDtypeStruct((rows, MLA_WIDTH), F32),
        jax.ShapeDtypeStruct((rows, SSD_WIDTH), F32),
        jax.ShapeDtypeStruct((rows, CONV_DIM), F32),
        jax.ShapeDtypeStruct((rows, LANES), F32),
    )
    out_specs = (
        pl.BlockSpec((MLA_HEADS, tr, QCAT), lambda i: (0, i, 0)),
        row(KV_LORA), row(ROPE_DIM), row(QCAT), row(MLA_WIDTH), row(SSD_WIDTH), row(CONV_DIM), row(LANES),
    )
    return pl.pallas_call(
        _pre_kernel, grid=(rows // tr,), in_specs=in_specs, out_specs=out_specs, out_shape=out_shape,
        compiler_params=pltpu.CompilerParams(dimension_semantics=("arbitrary",), vmem_limit_bytes=VMEM_LIMIT),
        name="pre",
    )(x, *weights, cos2, sin2)


def _softmax_step(q, kblk, m_sc, l_sc, acc_sc, allowed):
    s = lax.dot_general(q, kblk, (((1,), (1,)), ((), ())), preferred_element_type=F32) * ATTN_SCALE
    if allowed is not None:
        s = jnp.where(allowed, s, -jnp.inf)
    m_prev = m_sc[...]
    m_new = jnp.maximum(m_prev, jnp.max(s, axis=-1, keepdims=True))
    alpha = jnp.exp(m_prev - m_new)
    p = jnp.exp(s - m_new)
    l_sc[...] = alpha * l_sc[...] + jnp.sum(p, axis=-1, keepdims=True)
    acc_sc[...] = alpha * acc_sc[...] + _dot(p.astype(BF16), kblk[:, :KV_LORA])
    m_sc[...] = m_new


def _attn_finish(l_sc, acc_sc, g_ref, wuv_ref, o_ref, tq):
    inv = 1.0 / l_sc[...]
    for hd in range(MLA_HEADS):
        rows = slice(hd * tq, (hd + 1) * tq)
        cols = slice(hd * V_DIM, (hd + 1) * V_DIM)
        o = _dot((acc_sc[rows, :] * inv[rows, :]).astype(BF16), wuv_ref[hd])
        o_ref[:, cols] = (o * _silu(g_ref[:, cols])).astype(o_ref.dtype)


def _attn_init(m_sc, l_sc, acc_sc):
    m_sc[...] = jnp.full(m_sc.shape, -jnp.inf, F32)
    l_sc[...] = jnp.zeros(l_sc.shape, F32)
    acc_sc[...] = jnp.zeros(acc_sc.shape, F32)


def _attn_prompt_kernel(q_ref, k_ref, g_ref, wuv_ref, o_ref, m_sc, l_sc, acc_sc, *, tq, tk):
    i = pl.program_id(0)
    q = q_ref[...].reshape(MLA_HEADS * tq, QCAT)
    _attn_init(m_sc, l_sc, acc_sc)
    nfull = (i * tq) // tk

    def body(j, carry):
        kblk = k_ref[pl.ds(pl.multiple_of(j * tk, tk), tk), :]
        _softmax_step(q, kblk, m_sc, l_sc, acc_sc, None)
        return carry

    lax.fori_loop(0, nfull, body, 0)
    j0 = pl.multiple_of(nfull * tk, tk)
    rows = MLA_HEADS * tq
    qpos = i * tq + lax.broadcasted_iota(jnp.int32, (rows, 1), 0) % tq
    kpos = j0 + lax.broadcasted_iota(jnp.int32, (1, tk), 1)
    allowed = (kpos // CHUNK) <= (qpos // CHUNK)
    _softmax_step(q, k_ref[pl.ds(j0, tk), :], m_sc, l_sc, acc_sc, allowed)
    _attn_finish(l_sc, acc_sc, g_ref, wuv_ref, o_ref, tq)


def _attn_prompt_call(qcat, kcat, g, wuv_t, tq, tk):
    _, rows, _ = qcat.shape
    assert tq % CHUNK == 0 and tk % tq == 0 and rows % tk == 0
    r8 = MLA_HEADS * tq
    return pl.pallas_call(
        functools.partial(_attn_prompt_kernel, tq=tq, tk=tk),
        grid=(rows // tq,),
        in_specs=[pl.BlockSpec((MLA_HEADS, tq, QCAT), lambda i: (0, i, 0)),
                  _const_spec(kcat.shape),
                  pl.BlockSpec((tq, MLA_WIDTH), lambda i: (i, 0)),
                  _const_spec(wuv_t.shape)],
        out_specs=pl.BlockSpec((tq, MLA_WIDTH), lambda i: (i, 0)),
        out_shape=jax.ShapeDtypeStruct((rows, MLA_WIDTH), BF16),
        scratch_shapes=[pltpu.VMEM((r8, 1), F32), pltpu.VMEM((r8, 1), F32), pltpu.VMEM((r8, KV_LORA), F32)],
        compiler_params=pltpu.CompilerParams(dimension_semantics=("arbitrary",), vmem_limit_bytes=VMEM_LIMIT),
        name="attn_prompt",
    )(qcat, kcat, g, wuv_t)


def _attn_sample_kernel(q_ref, pckv_ref, pkpe_ref, knew_ref, g_ref, wuv_ref, o_ref, m_sc, l_sc, acc_sc, *, tq, tk):
    q = q_ref[...].reshape(MLA_HEADS * tq, QCAT)
    _attn_init(m_sc, l_sc, acc_sc)
    past = pckv_ref.shape[0]

    def body(j, carry):
        r0 = pl.multiple_of(j * tk, tk)
        kpe = pkpe_ref[pl.ds(r0, tk), :]
        kblk = jnp.concatenate([pckv_ref[pl.ds(r0, tk), :], kpe, kpe], axis=-1).astype(BF16)
        _softmax_step(q, kblk, m_sc, l_sc, acc_sc, None)
        return carry

    lax.fori_loop(0, past // tk, body, 0)
    _softmax_step(q, knew_ref[...], m_sc, l_sc, acc_sc, None)
    _attn_finish(l_sc, acc_sc, g_ref, wuv_ref, o_ref, tq)


def _attn_sample_call(qcat, past_ckv, past_kpe, kcat, g, wuv_t, batch, length, tk):
    past = past_ckv.shape[1]
    assert past % CHUNK == 0 and length <= CHUNK and past % tk == 0
    rows = batch * length
    r8 = MLA_HEADS * length
    return pl.pallas_call(
        functools.partial(_attn_sample_kernel, tq=length, tk=tk),
        grid=(batch,),
        in_specs=[pl.BlockSpec((MLA_HEADS, length, QCAT), lambda b: (0, b, 0)),
                  pl.BlockSpec((None, past, KV_LORA), lambda b: (b, 0, 0)),
                  pl.BlockSpec((None, past, ROPE_DIM), lambda b: (b, 0, 0)),
                  pl.BlockSpec((length, QCAT), lambda b: (b, 0)),
                  pl.BlockSpec((length, MLA_WIDTH), lambda b: (b, 0)),
                  _const_spec(wuv_t.shape)],
        out_specs=pl.BlockSpec((length, MLA_WIDTH), lambda b: (b, 0)),
        out_shape=jax.ShapeDtypeStruct((rows, MLA_WIDTH), BF16),
        scratch_shapes=[pltpu.VMEM((r8, 1), F32), pltpu.VMEM((r8, 1), F32), pltpu.VMEM((r8, KV_LORA), F32)],
        compiler_params=pltpu.CompilerParams(dimension_semantics=("arbitrary",), vmem_limit_bytes=VMEM_LIMIT),
        name="attn_sample",
    )(qcat, past_ckv, past_kpe, kcat, g, wuv_t)


def _ssd_constants(cl):
    hb = LANES // cl
    ltri = np.tril(np.ones((cl, cl), np.float32))
    exp_s = np.zeros((LANES, SSD_HEADS * cl), np.float32)
    exp_p = np.zeros((LANES, SSD_WIDTH), np.float32)
    for hd in range(SSD_HEADS):
        exp_s[hd, hd * cl:(hd + 1) * cl] = 1.0
        exp_p[hd, hd * SSD_HEAD_DIM:(hd + 1) * SSD_HEAD_DIM] = 1.0
    t = np.arange(cl)[:, None]
    s = np.tile(np.arange(cl), SSD_HEADS)[None, :]
    later = (t > s).astype(np.float32)
    tril = (t >= s).astype(np.float32)
    hrow = np.repeat(np.arange(hb), cl)[:, None]
    hcol = np.repeat(np.arange(hb), SSD_HEAD_DIM)[None, :]
    bdiag = (hrow == hcol).astype(np.float32)
    return tuple(jnp.asarray(a) for a in (ltri, exp_s, exp_p, later, tril, bdiag))


def _ssd_kernel(xbc_ref, dt_ref, z_ref, cst_ref, h0_ref, cw_ref, cb_ref, dtb_ref, alog_ref, dsk_ref, nw_ref,
                ltri_ref, exps_ref, expp_ref, later_ref, tril_ref, bdiag_ref,
                y_ref, cnew_ref, hout_ref, buf_sc, act_sc, h_sc, *, cl, cps):
    step = pl.program_id(1)
    last = pl.num_programs(1) - 1
    rows = cl * cps
    keep = CONV_W - 1
    base = SUBLANES - keep

    @pl.when(step == 0)
    def _():
        buf_sc[base:SUBLANES, :] = cst_ref[...]
        h_sc[...] = h0_ref[...]

    buf_sc[SUBLANES:SUBLANES + rows, :] = xbc_ref[...]
    conv = cb_ref[...] + cw_ref[0:1, :] * buf_sc[base:base + rows, :]
    for k in range(1, CONV_W):
        conv = conv + cw_ref[k:k + 1, :] * buf_sc[base + k:base + k + rows, :]
    act_sc[...] = _silu(conv)
    tail = buf_sc[base + rows:SUBLANES + rows, :]
    buf_sc[base:SUBLANES, :] = tail

    @pl.when(step == last)
    def _():
        cnew_ref[...] = tail

    a_neg = -jnp.exp(alog_ref[...])
    hb = LANES // cl
    gs = GROUP_WIDTH
    gm = HEADS_PER_GROUP * cl

    def chunk(c, carry):
        r0 = pl.multiple_of(c * cl, cl)
        xs = act_sc[pl.ds(r0, cl), 0:SSD_WIDTH]
        bm = act_sc[pl.ds(r0, cl), SSD_WIDTH:SSD_WIDTH + SSD_GROUPS * SSD_STATE]
        cm = act_sc[pl.ds(r0, cl), SSD_WIDTH + SSD_GROUPS * SSD_STATE:CONV_DIM]
        x = dt_ref[pl.ds(r0, cl), :] + dtb_ref[...]
        dt = jnp.maximum(x, 0.0) + jnp.log1p(jnp.exp(-jnp.abs(x)))
        a = dt * a_neg
        ltri = ltri_ref[...]
        cum = _dot_exact(ltri, a)
        total = cum[cl - 1:cl, :]
        seg = _dot_exact(ltri, _dot_exact(a, exps_ref[...]) * later_ref[...])
        decay = jnp.where(tril_ref[...] > 0.0, jnp.exp(seg), 0.0)
        wide = _dot_exact(jnp.concatenate([dt, jnp.exp(total - cum), jnp.exp(cum)], axis=0), expp_ref[...])
        dt_w, dstate_w, ecum_w = wide[0:cl], wide[cl:2 * cl], wide[2 * cl:3 * cl]
        xd = xs * dt_w
        w_state = (xd * dstate_w).astype(BF16)
        xd16 = xd.astype(BF16)
        h_prev = h_sc[...]
        h16 = h_prev.astype(BF16)
        bdiag = bdiag_ref[...].astype(BF16)
        y_parts, st_parts = [], []
        for g in range(SSD_GROUPS):
            bg = bm[:, g * SSD_STATE:(g + 1) * SSD_STATE].astype(BF16)
            cg = cm[:, g * SSD_STATE:(g + 1) * SSD_STATE].astype(BF16)
            b_rep = jnp.concatenate([bg] * HEADS_PER_GROUP, axis=0)
            cb = lax.dot_general(cg, b_rep, (((1,), (1,)), ((), ())), preferred_element_type=F32)
            m = (cb * decay[:, g * gm:(g + 1) * gm]).astype(BF16)
            y_off = _dot(cg, h16[:, g * gs:(g + 1) * gs]) * ecum_w[:, g * gs:(g + 1) * gs]
            diag = []
            for j in range(gm // LANES):
                c0 = g * gs + j * hb * SSD_HEAD_DIM
                xj = xd16[:, c0:c0 + hb * SSD_HEAD_DIM]
                blockdiag = jnp.concatenate([xj] * hb, axis=0) * bdiag
                diag.append(_dot(m[:, j * LANES:(j + 1) * LANES], blockdiag))
            y_parts.append(y_off + jnp.concatenate(diag, axis=-1))
            st_parts.append(lax.dot_general(bg, w_state[:, g * gs:(g + 1) * gs], (((0,), (0,)), ((), ())),
                                            preferred_element_type=F32))
        h_sc[...] = h_prev * ecum_w[cl - 1:cl, :] + jnp.concatenate(st_parts, axis=-1)
        y = jnp.concatenate(y_parts, axis=-1) + xs * dsk_ref[...]
        y = y * _silu(z_ref[pl.ds(r0, cl), :])
        normed = [_rms(y[:, g * gs:(g + 1) * gs], nw_ref[:, g * gs:(g + 1) * gs]) for g in range(SSD_GROUPS)]
        y_ref[pl.ds(r0, cl), :] = jnp.concatenate(normed, axis=-1).astype(y_ref.dtype)
        return carry

    lax.fori_loop(0, cps, chunk, 0)

    @pl.when(step == last)
    def _():
        hout_ref[...] = h_sc[...]


def _ssd_call(xbc, dt_raw, z, conv_state, h0_t, w, batch, length, cps):
    cl = min(CHUNK, length)
    rows = cl * cps
    assert length % rows == 0 and LANES % cl == 0 and rows >= CONV_W - 1
    consts = _ssd_constants(cl)
    seq = lambda n: pl.BlockSpec((None, rows, n), lambda b, s: (b, s, 0))
    per_batch = lambda shape: pl.BlockSpec((None,) + shape, lambda b, s: (b, 0, 0))
    params = (w["conv_w"], w["conv_b"], w["dt_bias"], w["a_log"], w["d_skip"], w["ssd_norm_w"])
    in_specs = ([seq(CONV_DIM), seq(LANES), seq(SSD_WIDTH),
                 per_batch((CONV_W - 1, CONV_DIM)), per_batch((SSD_STATE, SSD_WIDTH))]
                + [pl.BlockSpec(a.shape, lambda b, s: (0, 0)) for a in params + consts])
    out_shape = (jax.ShapeDtypeStruct((batch, length, SSD_WIDTH), BF16),
                 jax.ShapeDtypeStruct((batch, CONV_W - 1, CONV_DIM), F32),
                 jax.ShapeDtypeStruct((batch, SSD_STATE, SSD_WIDTH), F32))
    out_specs = (seq(SSD_WIDTH), per_batch((CONV_W - 1, CONV_DIM)), per_batch((SSD_STATE, SSD_WIDTH)))
    return pl.pallas_call(
        functools.partial(_ssd_kernel, cl=cl, cps=cps),
        grid=(batch, length // rows), in_specs=in_specs, out_specs=out_specs, out_shape=out_shape,
        scratch_shapes=[pltpu.VMEM((rows + SUBLANES, CONV_DIM), F32), pltpu.VMEM((rows, CONV_DIM), F32),
                        pltpu.VMEM((SSD_STATE, SSD_WIDTH), F32)],
        compiler_params=pltpu.CompilerParams(dimension_semantics=("arbitrary", "arbitrary"),
                                             vmem_limit_bytes=VMEM_LIMIT),
        name="ssd",
    )(xbc.reshape(batch, length, CONV_DIM), dt_raw.reshape(batch, length, LANES),
      z.reshape(batch, length, SSD_WIDTH), conv_state, h0_t, *params, *consts)


def _post_kernel(mla_ref, ssd_ref, x_ref, wa_ref, wb_ref, fw_ref, o_ref, *, final):
    y = x_ref[...] + _dot(mla_ref[...], wa_ref[...]) + _dot(ssd_ref[...], wb_ref[...])
    o_ref[...] = _rms(y, fw_ref[...]) if final else y


def _post_call(mla, ssd, x, w, final_norm_w, final, tr):
    rows, d = x.shape
    row = lambda n: pl.BlockSpec((tr, n), lambda i: (i, 0))
    return pl.pallas_call(
        functools.partial(_post_kernel, final=final),
        grid=(rows // tr,),
        in_specs=[row(MLA_WIDTH), row(SSD_WIDTH), row(d), _const_spec(w["w_out_mla"].shape),
                  _const_spec(w["w_out_ssd"].shape), _const_spec(final_norm_w.shape)],
        out_specs=row(d), out_shape=jax.ShapeDtypeStruct((rows, d), F32),
        compiler_params=pltpu.CompilerParams(dimension_semantics=("arbitrary",), vmem_limit_bytes=VMEM_LIMIT),
        name="post",
    )(mla, ssd, x, w["w_out_mla"], w["w_out_ssd"], final_norm_w)


def _swap_halves(wcols):
    k, n = wcols.shape
    blocks = wcols.reshape(k, n // ROPE_DIM, 2, ROPE_DIM // 2)
    return blocks[:, :, ::-1, :].reshape(k, n)


def _layer_weights(i, norm_w, w_in, q_norm_w, w_uq, kv_norm_w, w_uk, w_uv, conv_w, conv_b, dt_bias, a_log, d_skip,
                   ssd_norm_w, w_out):
    d = w_in.shape[1]
    offs = np.cumsum((0, Q_LORA, KV_LORA, ROPE_DIM, MLA_WIDTH, SSD_WIDTH, CONV_DIM, SSD_HEADS))
    col = lambda j: w_in[i][:, offs[j]:offs[j + 1]]
    w_kpe = col(2)
    w_kpe_sw = _swap_halves(w_kpe)
    w_sm = jnp.concatenate([w_kpe, w_kpe, w_kpe_sw, w_kpe_sw, col(6), jnp.zeros((d, LANES - SSD_HEADS), F32)], axis=1)
    uq = w_uq[i].reshape(Q_LORA, MLA_HEADS, NOPE_DIM + ROPE_DIM)
    uq_pe = uq[:, :, NOPE_DIM:].reshape(Q_LORA, MLA_HEADS * ROPE_DIM)
    pad_heads = lambda v: jnp.concatenate([v, jnp.zeros((LANES - SSD_HEADS,), F32)])[None, :]
    return {
        "norm_w": norm_w[i][None, :],
        "w_q": col(0).astype(BF16), "w_kv": col(1).astype(BF16), "w_g": col(3).astype(BF16),
        "w_z": col(4).astype(BF16), "w_xbc": col(5).astype(BF16), "w_sm": w_sm.astype(BF16),
        "q_norm_w": q_norm_w[i][None, :],
        "w_uq_nope": uq[:, :, :NOPE_DIM].reshape(Q_LORA, MLA_HEADS * NOPE_DIM).astype(BF16),
        "w_uq_pe": uq_pe.astype(BF16), "w_uq_pe_sw": _swap_halves(uq_pe).astype(BF16),
        "kv_norm_w": kv_norm_w[i][None, :],
        "w_uk_t": jnp.transpose(w_uk[i], (1, 2, 0)).astype(BF16),
        "w_uv_t": jnp.transpose(w_uv[i], (1, 0, 2)).astype(BF16),
        "conv_w": conv_w[i], "conv_b": conv_b[i][None, :],
        "dt_bias": pad_heads(dt_bias[i]), "a_log": pad_heads(a_log[i]),
        "d_skip": jnp.repeat(d_skip[i], SSD_HEAD_DIM)[None, :],
        "ssd_norm_w": ssd_norm_w[i][None, :],
        "w_out_mla": w_out[i][:MLA_WIDTH].astype(BF16), "w_out_ssd": w_out[i][MLA_WIDTH:].astype(BF16),
    }


def _rope_tables(past, length, batch):
    half = ROPE_DIM // 2
    inv = 1.0 / (ROPE_THETA ** (jnp.arange(half, dtype=F32) / half))
    ang = (past + jnp.arange(length)).astype(F32)[:, None] * inv[None, :]
    cos, sin = jnp.cos(ang), jnp.sin(ang)
    reps = LANES // ROPE_DIM
    cos2 = jnp.tile(jnp.concatenate([cos, cos], axis=-1), (batch, reps))
    sin2 = jnp.tile(jnp.concatenate([-sin, sin], axis=-1), (batch, reps))
    return cos2, sin2


def _state_to_lanes(h):
    b = h.shape[0]
    return jnp.transpose(h, (0, 3, 1, 2)).reshape(b, SSD_STATE, SSD_WIDTH)


def _state_from_lanes(ht):
    b = ht.shape[0]
    return jnp.transpose(ht.reshape(b, SSD_STATE, SSD_HEADS, SSD_HEAD_DIM), (0, 2, 3, 1))


def _pick(n, prefs):
    for p in prefs:
        if n % p == 0:
            return p
    return n


def _layer(x, batch, length, past_ckv, past_kpe, conv_state, ssm_state, w, tables, final_norm_w, final):
    rows = batch * length
    tr = _pick(rows, (256, 128))
    qcat, ckv, kpe, kcat, g, z, xbc, dt_raw = _pre_call(x, w, tables[0], tables[1], tr)
    if past_ckv is None:
        assert batch == 1
        tq = _pick(length, (128, 64))
        mla = _attn_prompt_call(qcat, kcat, g, w["w_uv_t"], tq, _pick(length, (512, 256, 128, 64)))
    else:
        mla = _attn_sample_call(qcat, past_ckv, past_kpe, kcat, g, w["w_uv_t"], batch, length,
                                _pick(past_ckv.shape[1], (512, 256, 128, 64)))
    cl = min(CHUNK, length)
    cps = _pick(length // cl, (4, 2, 1))
    ssd, conv_new, h_t = _ssd_call(xbc, dt_raw, z, conv_state, _state_to_lanes(ssm_state), w, batch, length, cps)
    y = _post_call(mla, ssd.reshape(rows, SSD_WIDTH), x, w, final_norm_w, final, tr)
    return (y, ckv.reshape(batch, length, KV_LORA), kpe.reshape(batch, length, ROPE_DIM), conv_new,
            _state_from_lanes(h_t))


def kernel(x_prompt, x_sample, cache_ckv, cache_kpe, state_conv, state_ssm, norm_w, w_in, q_norm_w, w_uq,
           kv_norm_w, w_uk, w_uv, conv_w, conv_b, dt_bias, a_log, d_skip, ssd_norm_w, w_out, final_norm_w):
    depth = w_in.shape[0]
    bp, lp, d = x_prompt.shape
    bs, ls, _ = x_sample.shape
    past = cache_ckv.shape[2]
    tab_p = _rope_tables(0, lp, bp)
    tab_s = _rope_tables(past, ls, bs)
    fw = final_norm_w[None, :]
    yp, ys = x_prompt.reshape(bp * lp, d), x_sample.reshape(bs * ls, d)
    outs_p, outs_s = [], []
    for i in range(depth):
        w = _layer_weights(i, norm_w, w_in, q_norm_w, w_uq, kv_norm_w, w_uk, w_uv, conv_w, conv_b, dt_bias, a_log,
                           d_skip, ssd_norm_w, w_out)
        final = i == depth - 1
        yp, *new_p = _layer(yp, bp, lp, None, None, jnp.zeros((bp, CONV_W - 1, CONV_DIM), F32),
                            jnp.zeros((bp, SSD_HEADS, SSD_HEAD_DIM, SSD_STATE), F32), w, tab_p, fw, final)
        ys, *new_s = _layer(ys, bs, ls, cache_ckv[i], cache_kpe[i], state_conv[i], state_ssm[i], w, tab_s, fw, final)
        outs_p.append(new_p)
        outs_s.append(new_s)
    stack = lambda outs, j: jnp.stack([o[j] for o in outs])
    return (yp.reshape(bp, lp, d), ys.reshape(bs, ls, d),
            stack(outs_p, 0), stack(outs_p, 1), stack(outs_p, 2), stack(outs_p, 3),
            stack(outs_s, 0), stack(outs_s, 1), stack(outs_s, 2), stack(outs_s, 3))
```

```python
import functools

import numpy as np
import jax
import jax.numpy as jnp
from jax import lax
from jax.experimental import pallas as pl
from jax.experimental.pallas import tpu as pltpu

F32 = jnp.float32
BF16 = jnp.bfloat16

EPS = 1e-6
CHUNK = 64
MLA_HEADS = 8
Q_LORA = 384
KV_LORA = 256
NOPE_DIM = 128
ROPE_DIM = 64
V_DIM = 128
ROPE_THETA = 10000.0
ATTN_SCALE = (NOPE_DIM + ROPE_DIM) ** -0.5
Q_SCALE = ATTN_SCALE * float(np.log2(np.e))
SSD_HEADS = 16
SSD_HEAD_DIM = 64
SSD_GROUPS = 2
SSD_STATE = 128
CONV_W = 4
SSD_WIDTH = SSD_HEADS * SSD_HEAD_DIM
GROUP_WIDTH = SSD_WIDTH // SSD_GROUPS
HEADS_PER_GROUP = SSD_HEADS // SSD_GROUPS
CONV_DIM = SSD_WIDTH + 2 * SSD_GROUPS * SSD_STATE
MLA_WIDTH = MLA_HEADS * V_DIM
QCAT = KV_LORA + 2 * ROPE_DIM

LANES = 128
SUBLANES = 8
VMEM_LIMIT = 56 * 1024 * 1024


def _dot(a, b):
    return jnp.dot(a, b, preferred_element_type=F32)


def _dot_exact(a, b):
    return jnp.dot(a, b, preferred_element_type=F32, precision=lax.Precision.HIGHEST)


def _rms(x, w):
    return x * lax.rsqrt(jnp.mean(x * x, axis=-1, keepdims=True) + EPS) * w


def _silu(x):
    return x * (1.0 / (1.0 + jnp.exp(-x)))


def _const_spec(shape):
    nd = len(shape)
    return pl.BlockSpec(shape, lambda *_: (0,) * nd)


def _pre_kernel(x_ref, nw_ref, wq_ref, wkv_ref, wg_ref, wz_ref, wxbc_ref, wsm_ref,
                qnw_ref, wuqn_ref, wuqp_ref, wuqps_ref, kvnw_ref, wuk_ref, cos_ref, sin_ref,
                qcat_ref, ckv_ref, kpe_ref, kcat_ref, g_ref, z_ref, xbc_ref, dt_ref):
    x = x_ref[...]
    h = _rms(x, nw_ref[...]).astype(BF16)
    g_ref[...] = _dot(h, wg_ref[...])
    z_ref[...] = _dot(h, wz_ref[...])
    xbc_ref[...] = _dot(h, wxbc_ref[...])
    cos2 = cos_ref[...]
    sin2 = sin_ref[...]
    sm = _dot(h, wsm_ref[...])
    kpe2 = sm[:, 0:LANES] * cos2 + sm[:, LANES:2 * LANES] * sin2
    dt_ref[...] = sm[:, 2 * LANES:3 * LANES]
    kpe_ref[...] = kpe2[:, :ROPE_DIM]
    ckv = _rms(_dot(h, wkv_ref[...]), kvnw_ref[...])
    ckv_ref[...] = ckv
    kcat_ref[...] = jnp.concatenate([ckv, kpe2], axis=-1).astype(BF16)

    qn = _rms(_dot(h, wq_ref[...]), qnw_ref[...]).astype(BF16)
    qnope = _dot(qn, wuqn_ref[...])
    pairs = MLA_HEADS // 2
    cos_all = jnp.concatenate([cos2] * pairs, axis=-1)
    sin_all = jnp.concatenate([sin2] * pairs, axis=-1)
    qpe = _dot(qn, wuqp_ref[...]) * cos_all + _dot(qn, wuqps_ref[...]) * sin_all
    lane = lax.broadcasted_iota(jnp.int32, (x.shape[0], LANES), 1)
    for hd in range(MLA_HEADS):
        qlat = _dot(qnope[:, hd * NOPE_DIM:(hd + 1) * NOPE_DIM].astype(BF16), wuk_ref[hd]) * Q_SCALE
        pair = qpe[:, (hd // 2) * LANES:(hd // 2 + 1) * LANES] * Q_SCALE
        keep = (lane < ROPE_DIM) if hd % 2 == 0 else (lane >= ROPE_DIM)
        qcat_ref[hd] = jnp.concatenate([qlat, jnp.where(keep, pair, 0.0)], axis=-1).astype(BF16)


def _pre_call(x, w, cos2, sin2, tr):
    rows, d = x.shape
    row = lambda n: pl.BlockSpec((tr, n), lambda i: (i, 0))
    weights = (w["norm_w"], w["w_q"], w["w_kv"], w["w_g"], w["w_z"], w["w_xbc"], w["w_sm"],
               w["q_norm_w"], w["w_uq_nope"], w["w_uq_pe"], w["w_uq_pe_sw"], w["kv_norm_w"], w["w_uk_t"])
    in_specs = [row(d)] + [_const_spec(a.shape) for a in weights] + [row(LANES), row(LANES)]
    out_shape = (
        jax.ShapeDtypeStruct((MLA_HEADS, rows, QCAT), BF16),
        jax.ShapeDtypeStruct((rows, KV_LORA), F32),
        jax.ShapeDtypeStruct((rows, ROPE_DIM), F32),
        jax.ShapeDtypeStruct((rows, QCAT), BF16),
        jax.ShapeDtypeStruct((rows, MLA_WIDTH), F32),
        jax.ShapeDtypeStruct((rows, SSD_WIDTH), F32),
        jax.ShapeDtypeStruct((rows, CONV_DIM), F32),
        jax.ShapeDtypeStruct((rows, LANES), F32),
    )
    out_specs = (
        pl.BlockSpec((MLA_HEADS, tr, QCAT), lambda i: (0, i, 0)),
        row(KV_LORA), row(ROPE_DIM), row(QCAT), row(MLA_WIDTH), row(SSD_WIDTH), row(CONV_DIM), row(LANES),
    )
    return pl.pallas_call(
        _pre_kernel, grid=(rows // tr,), in_specs=in_specs, out_specs=out_specs, out_shape=out_shape,
        compiler_params=pltpu.CompilerParams(dimension_semantics=("arbitrary",), vmem_limit_bytes=VMEM_LIMIT),
        name="pre",
    )(x, *weights, cos2, sin2)


def _scores(q, kblk):
    return lax.dot_general(q, kblk, (((1,), (1,)), ((), ())), preferred_element_type=F32)


def _softmax_pv(s, v, m_sc, l_sc, acc_sc, allowed):
    rows = slice(0, s.shape[0])
    if allowed is not None:
        s = jnp.where(allowed, s, -jnp.inf)
    tk = s.shape[1]
    m_prev = m_sc[rows, :]
    if tk % LANES == 0:
        chunks = [s[:, c * LANES:(c + 1) * LANES] for c in range(tk // LANES)]
        m_new = jnp.maximum(m_prev, jnp.max(functools.reduce(jnp.maximum, chunks), axis=-1, keepdims=True))
        ps = [jnp.exp2(c - m_new) for c in chunks]
        psum = functools.reduce(jnp.add, ps)
        p = jnp.concatenate(ps, axis=-1)
    else:
        m_new = jnp.maximum(m_prev, jnp.max(s, axis=-1, keepdims=True))
        p = jnp.exp2(s - m_new[:, :1])
        psum = jnp.sum(p, axis=-1, keepdims=True) * (1.0 / LANES)
    alpha = jnp.exp2(m_prev - m_new)
    l_sc[rows, :] = alpha * l_sc[rows, :] + psum
    pv = _dot(p.astype(BF16), v)
    acc_sc[rows, :] = jnp.concatenate([alpha] * (KV_LORA // LANES), axis=-1) * acc_sc[rows, :] + pv
    m_sc[rows, :] = m_new


def _attn_finish(l_sc, acc_sc, g_ref, wuv_ref, o_ref, tq):
    for hd in range(MLA_HEADS):
        rows = slice(hd * tq, (hd + 1) * tq)
        cols = slice(hd * V_DIM, (hd + 1) * V_DIM)
        inv = 1.0 / jnp.sum(l_sc[rows, :], axis=-1, keepdims=True)
        o = _dot((acc_sc[rows, :] * inv).astype(BF16), wuv_ref[hd])
        o_ref[:, cols] = (o * _silu(g_ref[:, cols])).astype(o_ref.dtype)


def _attn_init(m_sc, l_sc, acc_sc):
    m_sc[...] = jnp.full(m_sc.shape, -jnp.inf, F32)
    l_sc[...] = jnp.zeros(l_sc.shape, F32)
    acc_sc[...] = jnp.zeros(acc_sc.shape, F32)


def _attn_scratch(rows):
    return [pltpu.VMEM((rows, LANES), F32), pltpu.VMEM((rows, LANES), F32), pltpu.VMEM((rows, KV_LORA), F32)]


def _attn_prompt_kernel(q_ref, k_ref, g_ref, wuv_ref, o_ref, m_sc, l_sc, acc_sc, sa_sc, sb_sc, *, tq):
    i = pl.program_id(0)
    _attn_init(m_sc, l_sc, acc_sc)
    q = q_ref[...].reshape(MLA_HEADS * tq, QCAT)

    def keys(j):
        return k_ref[pl.ds(pl.multiple_of(j * tq, tq), tq), :]

    def step(s_cur, s_next, j):
        s_next[...] = _scores(q, keys(j + 1))
        _softmax_pv(s_cur[...], keys(j)[:, :KV_LORA], m_sc, l_sc, acc_sc, None)

    def own_block(s_cur):
        qchunk = lax.broadcasted_iota(jnp.int32, (tq, 1), 0) // CHUNK
        kchunk = lax.broadcasted_iota(jnp.int32, (1, tq), 1) // CHUNK
        allowed = jnp.concatenate([kchunk <= qchunk] * MLA_HEADS, axis=0)
        _softmax_pv(s_cur[...], keys(i)[:, :KV_LORA], m_sc, l_sc, acc_sc, allowed)

    sa_sc[...] = _scores(q, keys(0))

    def body(jj, carry):
        step(sa_sc, sb_sc, 2 * jj)
        step(sb_sc, sa_sc, 2 * jj + 1)
        return carry

    lax.fori_loop(0, i // 2, body, 0)

    @pl.when(i % 2 == 1)
    def _():
        step(sa_sc, sb_sc, i - 1)
        own_block(sb_sc)

    @pl.when(i % 2 == 0)
    def _():
        own_block(sa_sc)

    _attn_finish(l_sc, acc_sc, g_ref, wuv_ref, o_ref, tq)


def _attn_prompt_call(qcat, kcat, g, wuv_t, tq):
    _, rows, _ = qcat.shape
    assert tq % CHUNK == 0 and rows % tq == 0
    r8 = MLA_HEADS * tq
    return pl.pallas_call(
        functools.partial(_attn_prompt_kernel, tq=tq),
        grid=(rows // tq,),
        in_specs=[pl.BlockSpec((MLA_HEADS, tq, QCAT), lambda i: (0, i, 0)),
                  _const_spec(kcat.shape),
                  pl.BlockSpec((tq, MLA_WIDTH), lambda i: (i, 0)),
                  _const_spec(wuv_t.shape)],
        out_specs=pl.BlockSpec((tq, MLA_WIDTH), lambda i: (i, 0)),
        out_shape=jax.ShapeDtypeStruct((rows, MLA_WIDTH), BF16),
        scratch_shapes=_attn_scratch(r8) + [pltpu.VMEM((r8, tq), F32), pltpu.VMEM((r8, tq), F32)],
        compiler_params=pltpu.CompilerParams(dimension_semantics=("arbitrary",), vmem_limit_bytes=VMEM_LIMIT),
        name="attn_prompt",
    )(qcat, kcat, g, wuv_t)


def _attn_sample_kernel(q_ref, pckv_ref, pkpe_ref, knew_ref, g_ref, wuv_ref, o_ref, m_sc, l_sc, acc_sc, *, tq, tk):
    q = q_ref[...].reshape(MLA_HEADS * tq, QCAT)
    _attn_init(m_sc, l_sc, acc_sc)
    past = pckv_ref.shape[0]

    def body(j, carry):
        r0 = pl.multiple_of(j * tk, tk)
        kpe = pkpe_ref[pl.ds(r0, tk), :]
        kblk = jnp.concatenate([pckv_ref[pl.ds(r0, tk), :], kpe, kpe], axis=-1).astype(BF16)
        _softmax_pv(_scores(q, kblk), kblk[:, :KV_LORA], m_sc, l_sc, acc_sc, None)
        return carry

    lax.fori_loop(0, past // tk, body, 0)
    knew = knew_ref[...]
    _softmax_pv(_scores(q, knew), knew[:, :KV_LORA], m_sc, l_sc, acc_sc, None)
    _attn_finish(l_sc, acc_sc, g_ref, wuv_ref, o_ref, tq)


def _attn_sample_call(qcat, past_ckv, past_kpe, kcat, g, wuv_t, batch, length, tk):
    past = past_ckv.shape[1]
    assert past % CHUNK == 0 and length <= CHUNK and past % tk == 0
    rows = batch * length
    r8 = MLA_HEADS * length
    return pl.pallas_call(
        functools.partial(_attn_sample_kernel, tq=length, tk=tk),
        grid=(batch,),
        in_specs=[pl.BlockSpec((MLA_HEADS, length, QCAT), lambda b: (0, b, 0)),
                  pl.BlockSpec((None, past, KV_LORA), lambda b: (b, 0, 0)),
                  pl.BlockSpec((None, past, ROPE_DIM), lambda b: (b, 0, 0)),
                  pl.BlockSpec((length, QCAT), lambda b: (b, 0)),
                  pl.BlockSpec((length, MLA_WIDTH), lambda b: (b, 0)),
                  _const_spec(wuv_t.shape)],
        out_specs=pl.BlockSpec((length, MLA_WIDTH), lambda b: (b, 0)),
        out_shape=jax.ShapeDtypeStruct((rows, MLA_WIDTH), BF16),
        scratch_shapes=_attn_scratch(r8),
        compiler_params=pltpu.CompilerParams(dimension_semantics=("arbitrary",), vmem_limit_bytes=VMEM_LIMIT),
        name="attn_sample",
    )(qcat, past_ckv, past_kpe, kcat, g, wuv_t)


def _ssd_constants(cl):
    hb = LANES // cl
    ltri = np.tril(np.ones((cl, cl), np.float32))
    exp_s = np.zeros((LANES, SSD_HEADS * cl), np.float32)
    exp_p = np.zeros((LANES, SSD_WIDTH), np.float32)
    for hd in range(SSD_HEADS):
        exp_s[hd, hd * cl:(hd + 1) * cl] = 1.0
        exp_p[hd, hd * SSD_HEAD_DIM:(hd + 1) * SSD_HEAD_DIM] = 1.0
    t = np.arange(cl)[:, None]
    s = np.tile(np.arange(cl), SSD_HEADS)[None, :]
    later = (t > s).astype(np.float32)
    tril = (t >= s).astype(np.float32)
    hrow = np.repeat(np.arange(hb), cl)[:, None]
    hcol = np.repeat(np.arange(hb), SSD_HEAD_DIM)[None, :]
    bdiag = (hrow == hcol).astype(np.float32)
    return tuple(jnp.asarray(a) for a in (ltri, exp_s, exp_p, later, tril, bdiag))


def _ssd_kernel(xbc_ref, dt_ref, z_ref, cst_ref, h0_ref, cw_ref, cb_ref, dtb_ref, alog_ref, dsk_ref, nw_ref,
                ltri_ref, exps_ref, expp_ref, later_ref, tril_ref, bdiag_ref,
                y_ref, cnew_ref, hout_ref, buf_sc, act_sc, h_sc, *, cl, cps):
    step = pl.program_id(1)
    last = pl.num_programs(1) - 1
    rows = cl * cps
    keep = CONV_W - 1
    base = SUBLANES - keep

    @pl.when(step == 0)
    def _():
        buf_sc[base:SUBLANES, :] = cst_ref[...]
        h_sc[...] = h0_ref[...]

    buf_sc[SUBLANES:SUBLANES + rows, :] = xbc_ref[...]
    conv = cb_ref[...] + cw_ref[0:1, :] * buf_sc[base:base + rows, :]
    for k in range(1, CONV_W):
        conv = conv + cw_ref[k:k + 1, :] * buf_sc[base + k:base + k + rows, :]
    act_sc[...] = _silu(conv)
    tail = buf_sc[base + rows:SUBLANES + rows, :]
    buf_sc[base:SUBLANES, :] = tail

    @pl.when(step == last)
    def _():
        cnew_ref[...] = tail

    a_neg = -jnp.exp(alog_ref[...])
    hb = LANES // cl
    gs = GROUP_WIDTH
    gm = HEADS_PER_GROUP * cl

    def chunk(c, carry):
        r0 = pl.multiple_of(c * cl, cl)
        xs = act_sc[pl.ds(r0, cl), 0:SSD_WIDTH]
        bm = act_sc[pl.ds(r0, cl), SSD_WIDTH:SSD_WIDTH + SSD_GROUPS * SSD_STATE]
        cm = act_sc[pl.ds(r0, cl), SSD_WIDTH + SSD_GROUPS * SSD_STATE:CONV_DIM]
        x = dt_ref[pl.ds(r0, cl), :] + dtb_ref[...]
        dt = jnp.maximum(x, 0.0) + jnp.log1p(jnp.exp(-jnp.abs(x)))
        a = dt * a_neg
        ltri = ltri_ref[...]
        cum = _dot_exact(ltri, a)
        total = cum[cl - 1:cl, :]
        seg = _dot_exact(ltri, _dot_exact(a, exps_ref[...]) * later_ref[...])
        decay = jnp.where(tril_ref[...] > 0.0, jnp.exp(seg), 0.0)
        wide = _dot_exact(jnp.concatenate([dt, jnp.exp(total - cum), jnp.exp(cum)], axis=0), expp_ref[...])
        dt_w, dstate_w, ecum_w = wide[0:cl], wide[cl:2 * cl], wide[2 * cl:3 * cl]
        xd = xs * dt_w
        w_state = (xd * dstate_w).astype(BF16)
        xd16 = xd.astype(BF16)
        h_prev = h_sc[...]
        h16 = h_prev.astype(BF16)
        bdiag = bdiag_ref[...].astype(BF16)
        y_parts, st_parts = [], []
        for g in range(SSD_GROUPS):
            bg = bm[:, g * SSD_STATE:(g + 1) * SSD_STATE].astype(BF16)
            cg = cm[:, g * SSD_STATE:(g + 1) * SSD_STATE].astype(BF16)
            b_rep = jnp.concatenate([bg] * HEADS_PER_GROUP, axis=0)
            cb = lax.dot_general(cg, b_rep, (((1,), (1,)), ((), ())), preferred_element_type=F32)
            m = (cb * decay[:, g * gm:(g + 1) * gm]).astype(BF16)
            y_off = _dot(cg, h16[:, g * gs:(g + 1) * gs]) * ecum_w[:, g * gs:(g + 1) * gs]
            diag = []
            for j in range(gm // LANES):
                c0 = g * gs + j * hb * SSD_HEAD_DIM
                xj = xd16[:, c0:c0 + hb * SSD_HEAD_DIM]
                blockdiag = jnp.concatenate([xj] * hb, axis=0) * bdiag
                diag.append(_dot(m[:, j * LANES:(j + 1) * LANES], blockdiag))
            y_parts.append(y_off + jnp.concatenate(diag, axis=-1))
            st_parts.append(lax.dot_general(bg, w_state[:, g * gs:(g + 1) * gs], (((0,), (0,)), ((), ())),
                                            preferred_element_type=F32))
        h_sc[...] = h_prev * ecum_w[cl - 1:cl, :] + jnp.concatenate(st_parts, axis=-1)
        y = jnp.concatenate(y_parts, axis=-1) + xs * dsk_ref[...]
        y = y * _silu(z_ref[pl.ds(r0, cl), :])
        normed = [_rms(y[:, g * gs:(g + 1) * gs], nw_ref[:, g * gs:(g + 1) * gs]) for g in range(SSD_GROUPS)]
        y_ref[pl.ds(r0, cl), :] = jnp.concatenate(normed, axis=-1).astype(y_ref.dtype)
        return carry

    lax.fori_loop(0, cps, chunk, 0)

    @pl.when(step == last)
    def _():
        hout_ref[...] = h_sc[...]


def _ssd_call(xbc, dt_raw, z, conv_state, h0_t, w, batch, length, cps):
    cl = min(CHUNK, length)
    rows = cl * cps
    assert length % rows == 0 and LANES % cl == 0 and rows >= CONV_W - 1
    consts = _ssd_constants(cl)
    seq = lambda n: pl.BlockSpec((None, rows, n), lambda b, s: (b, s, 0))
    per_batch = lambda shape: pl.BlockSpec((None,) + shape, lambda b, s: (b, 0, 0))
    params = (w["conv_w"], w["conv_b"], w["dt_bias"], w["a_log"], w["d_skip"], w["ssd_norm_w"])
    in_specs = ([seq(CONV_DIM), seq(LANES), seq(SSD_WIDTH),
                 per_batch((CONV_W - 1, CONV_DIM)), per_batch((SSD_STATE, SSD_WIDTH))]
                + [pl.BlockSpec(a.shape, lambda b, s: (0, 0)) for a in params + consts])
    out_shape = (jax.ShapeDtypeStruct((batch, length, SSD_WIDTH), BF16),
                 jax.ShapeDtypeStruct((batch, CONV_W - 1, CONV_DIM), F32),
                 jax.ShapeDtypeStruct((batch, SSD_STATE, SSD_WIDTH), F32))
    out_specs = (seq(SSD_WIDTH), per_batch((CONV_W - 1, CONV_DIM)), per_batch((SSD_STATE, SSD_WIDTH)))
    return pl.pallas_call(
        functools.partial(_ssd_kernel, cl=cl, cps=cps),
        grid=(batch, length // rows), in_specs=in_specs, out_specs=out_specs, out_shape=out_shape,
        scratch_shapes=[pltpu.VMEM((rows + SUBLANES, CONV_DIM), F32), pltpu.VMEM((rows, CONV_DIM), F32),
                        pltpu.VMEM((SSD_STATE, SSD_WIDTH), F32)],
        compiler_params=pltpu.CompilerParams(dimension_semantics=("arbitrary", "arbitrary"),
                                             vmem_limit_bytes=VMEM_LIMIT),
        name="ssd",
    )(xbc.reshape(batch, length, CONV_DIM), dt_raw.reshape(batch, length, LANES),
      z.reshape(batch, length, SSD_WIDTH), conv_state, h0_t, *params, *consts)


def _post_kernel(mla_ref, ssd_ref, x_ref, wa_ref, wb_ref, fw_ref, o_ref, *, final):
    y = x_ref[...] + _dot(mla_ref[...], wa_ref[...]) + _dot(ssd_ref[...], wb_ref[...])
    o_ref[...] = _rms(y, fw_ref[...]) if final else y


def _post_call(mla, ssd, x, w, final_norm_w, final, tr):
    rows, d = x.shape
    row = lambda n: pl.BlockSpec((tr, n), lambda i: (i, 0))
    return pl.pallas_call(
        functools.partial(_post_kernel, final=final),
        grid=(rows // tr,),
        in_specs=[row(MLA_WIDTH), row(SSD_WIDTH), row(d), _const_spec(w["w_out_mla"].shape),
                  _const_spec(w["w_out_ssd"].shape), _const_spec(final_norm_w.shape)],
        out_specs=row(d), out_shape=jax.ShapeDtypeStruct((rows, d), F32),
        compiler_params=pltpu.CompilerParams(dimension_semantics=("arbitrary",), vmem_limit_bytes=VMEM_LIMIT),
        name="post",
    )(mla, ssd, x, w["w_out_mla"], w["w_out_ssd"], final_norm_w)


def _swap_halves(wcols):
    k, n = wcols.shape
    blocks = wcols.reshape(k, n // ROPE_DIM, 2, ROPE_DIM // 2)
    return blocks[:, :, ::-1, :].reshape(k, n)


def _layer_weights(i, norm_w, w_in, q_norm_w, w_uq, kv_norm_w, w_uk, w_uv, conv_w, conv_b, dt_bias, a_log, d_skip,
                   ssd_norm_w, w_out):
    d = w_in.shape[1]
    offs = np.cumsum((0, Q_LORA, KV_LORA, ROPE_DIM, MLA_WIDTH, SSD_WIDTH, CONV_DIM, SSD_HEADS))
    col = lambda j: w_in[i][:, offs[j]:offs[j + 1]]
    w_kpe = col(2)
    w_kpe_sw = _swap_halves(w_kpe)
    w_sm = jnp.concatenate([w_kpe, w_kpe, w_kpe_sw, w_kpe_sw, col(6), jnp.zeros((d, LANES - SSD_HEADS), F32)], axis=1)
    uq = w_uq[i].reshape(Q_LORA, MLA_HEADS, NOPE_DIM + ROPE_DIM)
    uq_pe = uq[:, :, NOPE_DIM:].reshape(Q_LORA, MLA_HEADS * ROPE_DIM)
    pad_heads = lambda v: jnp.concatenate([v, jnp.zeros((LANES - SSD_HEADS,), F32)])[None, :]
    return {
        "norm_w": norm_w[i][None, :],
        "w_q": col(0).astype(BF16), "w_kv": col(1).astype(BF16), "w_g": col(3).astype(BF16),
        "w_z": col(4).astype(BF16), "w_xbc": col(5).astype(BF16), "w_sm": w_sm.astype(BF16),
        "q_norm_w": q_norm_w[i][None, :],
        "w_uq_nope": uq[:, :, :NOPE_DIM].reshape(Q_LORA, MLA_HEADS * NOPE_DIM).astype(BF16),
        "w_uq_pe": uq_pe.astype(BF16), "w_uq_pe_sw": _swap_halves(uq_pe).astype(BF16),
        "kv_norm_w": kv_norm_w[i][None, :],
        "w_uk_t": jnp.transpose(w_uk[i], (1, 2, 0)).astype(BF16),
        "w_uv_t": jnp.transpose(w_uv[i], (1, 0, 2)).astype(BF16),
        "conv_w": conv_w[i], "conv_b": conv_b[i][None, :],
        "dt_bias": pad_heads(dt_bias[i]), "a_log": pad_heads(a_log[i]),
        "d_skip": jnp.repeat(d_skip[i], SSD_HEAD_DIM)[None, :],
        "ssd_norm_w": ssd_norm_w[i][None, :],
        "w_out_mla": w_out[i][:MLA_WIDTH].astype(BF16), "w_out_ssd": w_out[i][MLA_WIDTH:].astype(BF16),
    }


def _rope_tables(past, length, batch):
    half = ROPE_DIM // 2
    inv = 1.0 / (ROPE_THETA ** (jnp.arange(half, dtype=F32) / half))
    ang = (past + jnp.arange(length)).astype(F32)[:, None] * inv[None, :]
    cos, sin = jnp.cos(ang), jnp.sin(ang)
    reps = LANES // ROPE_DIM
    cos2 = jnp.tile(jnp.concatenate([cos, cos], axis=-1), (batch, reps))
    sin2 = jnp.tile(jnp.concatenate([-sin, sin], axis=-1), (batch, reps))
    return cos2, sin2


def _state_to_lanes(h):
    b = h.shape[0]
    return jnp.transpose(h, (0, 3, 1, 2)).reshape(b, SSD_STATE, SSD_WIDTH)


def _state_from_lanes(ht):
    b = ht.shape[0]
    return jnp.transpose(ht.reshape(b, SSD_STATE, SSD_HEADS, SSD_HEAD_DIM), (0, 2, 3, 1))


def _pick(n, prefs):
    for p in prefs:
        if n % p == 0:
            return p
    return n


def _layer(x, batch, length, past_ckv, past_kpe, conv_state, ssm_state, w, tables, final_norm_w, final):
    rows = batch * length
    tr = _pick(rows, (256, 128))
    qcat, ckv, kpe, kcat, g, z, xbc, dt_raw = _pre_call(x, w, tables[0], tables[1], tr)
    if past_ckv is None:
        assert batch == 1
        tq = _pick(length, (256, 128, 64))
        mla = _attn_prompt_call(qcat, kcat, g, w["w_uv_t"], tq)
    else:
        mla = _attn_sample_call(qcat, past_ckv, past_kpe, kcat, g, w["w_uv_t"], batch, length,
                                _pick(past_ckv.shape[1], (512, 256, 128, 64)))
    cl = min(CHUNK, length)
    cps = _pick(length // cl, (4, 2, 1))
    ssd, conv_new, h_t = _ssd_call(xbc, dt_raw, z, conv_state, _state_to_lanes(ssm_state), w, batch, length, cps)
    y = _post_call(mla, ssd.reshape(rows, SSD_WIDTH), x, w, final_norm_w, final, tr)
    return (y, ckv.reshape(batch, length, KV_LORA), kpe.reshape(batch, length, ROPE_DIM), conv_new,
            _state_from_lanes(h_t))


def kernel(x_prompt, x_sample, cache_ckv, cache_kpe, state_conv, state_ssm, norm_w, w_in, q_norm_w, w_uq,
           kv_norm_w, w_uk, w_uv, conv_w, conv_b, dt_bias, a_log, d_skip, ssd_norm_w, w_out, final_norm_w):
    depth = w_in.shape[0]
    bp, lp, d = x_prompt.shape
    bs, ls, _ = x_sample.shape
    past = cache_ckv.shape[2]
    tab_p = _rope_tables(0, lp, bp)
    tab_s = _rope_tables(past, ls, bs)
    fw = final_norm_w[None, :]
    yp, ys = x_prompt.reshape(bp * lp, d), x_sample.reshape(bs * ls, d)
    outs_p, outs_s = [], []
    for i in range(depth):
        w = _layer_weights(i, norm_w, w_in, q_norm_w, w_uq, kv_norm_w, w_uk, w_uv, conv_w, conv_b, dt_bias, a_log,
                           d_skip, ssd_norm_w, w_out)
        final = i == depth - 1
        yp, *new_p = _layer(yp, bp, lp, None, None, jnp.zeros((bp, CONV_W - 1, CONV_DIM), F32),
                            jnp.zeros((bp, SSD_HEADS, SSD_HEAD_DIM, SSD_STATE), F32), w, tab_p, fw, final)
        ys, *new_s = _layer(ys, bs, ls, cache_ckv[i], cache_kpe[i], state_conv[i], state_ssm[i], w, tab_s, fw, final)
        outs_p.append(new_p)
        outs_s.append(new_s)
    stack = lambda outs, j: jnp.stack([o[j] for o in outs])
    return (yp.reshape(bp, lp, d), ys.reshape(bs, ls, d),
            stack(outs_p, 0), stack(outs_p, 1), stack(outs_p, 2), stack(outs_p, 3),
            stack(outs_s, 0), stack(outs_s, 1), stack(outs_s, 2), stack(outs_s, 3))
```

```python
import functools

import numpy as np
import jax
import jax.numpy as jnp
from jax import lax
from jax.experimental import pallas as pl
from jax.experimental.pallas import tpu as pltpu

F32 = jnp.float32
BF16 = jnp.bfloat16

EPS = 1e-6
CHUNK = 64
MLA_HEADS = 8
Q_LORA = 384
KV_LORA = 256
NOPE_DIM = 128
ROPE_DIM = 64
V_DIM = 128
ROPE_THETA = 10000.0
ATTN_SCALE = (NOPE_DIM + ROPE_DIM) ** -0.5
Q_SCALE = ATTN_SCALE * float(np.log2(np.e))
SSD_HEADS = 16
SSD_HEAD_DIM = 64
SSD_GROUPS = 2
SSD_STATE = 128
CONV_W = 4
SSD_WIDTH = SSD_HEADS * SSD_HEAD_DIM
GROUP_WIDTH = SSD_WIDTH // SSD_GROUPS
HEADS_PER_GROUP = SSD_HEADS // SSD_GROUPS
CONV_DIM = SSD_WIDTH + 2 * SSD_GROUPS * SSD_STATE
MLA_WIDTH = MLA_HEADS * V_DIM
QCAT = KV_LORA + 2 * ROPE_DIM

CUM_PIECES = 3
WIDE_PIECES = 2

LANES = 128
SUBLANES = 8
VMEM_LIMIT = 56 * 1024 * 1024


def _dot(a, b):
    return jnp.dot(a, b, preferred_element_type=F32)


def _rms(x, w):
    return x * lax.rsqrt(jnp.mean(x * x, axis=-1, keepdims=True) + EPS) * w


def _silu(x):
    return x * (1.0 / (1.0 + jnp.exp(-x)))


def _const_spec(shape):
    nd = len(shape)
    return pl.BlockSpec(shape, lambda *_: (0,) * nd)


def _pre_kernel(x_ref, nw_ref, wq_ref, wkv_ref, wg_ref, wz_ref, wxbc_ref, wsm_ref,
                qnw_ref, wuqn_ref, wuqp_ref, wuqps_ref, kvnw_ref, wuk_ref, cos_ref, sin_ref,
                qcat_ref, ckv_ref, kpe_ref, kcat_ref, g_ref, z_ref, xbc_ref, dt_ref):
    x = x_ref[...]
    h = _rms(x, nw_ref[...]).astype(BF16)
    g_ref[...] = _dot(h, wg_ref[...])
    z_ref[...] = _dot(h, wz_ref[...])
    xbc_ref[...] = _dot(h, wxbc_ref[...])
    cos2 = cos_ref[...]
    sin2 = sin_ref[...]
    sm = _dot(h, wsm_ref[...])
    kpe2 = sm[:, 0:LANES] * cos2 + sm[:, LANES:2 * LANES] * sin2
    kpe_ref[...] = kpe2[:, :ROPE_DIM]
    ckv = _rms(_dot(h, wkv_ref[...]), kvnw_ref[...])
    ckv_ref[...] = ckv
    kcat_ref[...] = jnp.concatenate([ckv, kpe2], axis=-1).astype(BF16)

    qdt = _dot(h, wq_ref[...])
    dt_ref[...] = qdt[:, Q_LORA:Q_LORA + LANES]
    qn = _rms(qdt[:, :Q_LORA], qnw_ref[...]).astype(BF16)
    qnope = _dot(qn, wuqn_ref[...])
    pairs = MLA_HEADS // 2
    cos_all = jnp.concatenate([cos2] * pairs, axis=-1)
    sin_all = jnp.concatenate([sin2] * pairs, axis=-1)
    qpe = _dot(qn, wuqp_ref[...]) * cos_all + _dot(qn, wuqps_ref[...]) * sin_all
    lane = lax.broadcasted_iota(jnp.int32, (x.shape[0], LANES), 1)
    for hd in range(MLA_HEADS):
        qlat = _dot(qnope[:, hd * NOPE_DIM:(hd + 1) * NOPE_DIM].astype(BF16), wuk_ref[hd]) * Q_SCALE
        pair = qpe[:, (hd // 2) * LANES:(hd // 2 + 1) * LANES] * Q_SCALE
        keep = (lane < ROPE_DIM) if hd % 2 == 0 else (lane >= ROPE_DIM)
        qcat_ref[hd] = jnp.concatenate([qlat, jnp.where(keep, pair, 0.0)], axis=-1).astype(BF16)


def _pre_call(x, w, cos2, sin2, tr):
    rows, d = x.shape
    row = lambda n: pl.BlockSpec((tr, n), lambda i: (i, 0))
    weights = (w["norm_w"], w["w_q"], w["w_kv"], w["w_g"], w["w_z"], w["w_xbc"], w["w_sm"],
               w["q_norm_w"], w["w_uq_nope"], w["w_uq_pe"], w["w_uq_pe_sw"], w["kv_norm_w"], w["w_uk_t"])
    in_specs = [row(d)] + [_const_spec(a.shape) for a in weights] + [row(LANES), row(LANES)]
    out_shape = (
        jax.ShapeDtypeStruct((MLA_HEADS, rows, QCAT), BF16),
        jax.ShapeDtypeStruct((rows, KV_LORA), F32),
        jax.ShapeDtypeStruct((rows, ROPE_DIM), F32),
        jax.ShapeDtypeStruct((rows, QCAT), BF16),
        jax.ShapeDtypeStruct((rows, MLA_WIDTH), F32),
        jax.ShapeDtypeStruct((rows, SSD_WIDTH), F32),
        jax.ShapeDtypeStruct((rows, CONV_DIM), F32),
        jax.ShapeDtypeStruct((rows, LANES), F32),
    )
    out_specs = (
        pl.BlockSpec((MLA_HEADS, tr, QCAT), lambda i: (0, i, 0)),
        row(KV_LORA), row(ROPE_DIM), row(QCAT), row(MLA_WIDTH), row(SSD_WIDTH), row(CONV_DIM), row(LANES),
    )
    return pl.pallas_call(
        _pre_kernel, grid=(rows // tr,), in_specs=in_specs, out_specs=out_specs, out_shape=out_shape,
        compiler_params=pltpu.CompilerParams(dimension_semantics=("arbitrary",), vmem_limit_bytes=VMEM_LIMIT),
        name="pre",
    )(x, *weights, cos2, sin2)


def _scores(q, kblk):
    return lax.dot_general(q, kblk, (((1,), (1,)), ((), ())), preferred_element_type=F32)


def _softmax_pv(s, v, m_sc, l_sc, acc_sc, allowed):
    rows = slice(0, s.shape[0])
    if allowed is not None:
        s = jnp.where(allowed, s, -jnp.inf)
    tk = s.shape[1]
    m_prev = m_sc[rows, :]
    if tk % LANES == 0:
        chunks = [s[:, c * LANES:(c + 1) * LANES] for c in range(tk // LANES)]
        m_new = jnp.maximum(m_prev, jnp.max(functools.reduce(jnp.maximum, chunks), axis=-1, keepdims=True))
        ps = [jnp.exp2(c - m_new) for c in chunks]
        psum = functools.reduce(jnp.add, ps)
        p = jnp.concatenate(ps, axis=-1)
    else:
        m_new = jnp.maximum(m_prev, jnp.max(s, axis=-1, keepdims=True))
        p = jnp.exp2(s - m_new[:, :1])
        psum = jnp.sum(p, axis=-1, keepdims=True) * (1.0 / LANES)
    alpha = jnp.exp2(m_prev - m_new)
    l_sc[rows, :] = alpha * l_sc[rows, :] + psum
    pv = _dot(p.astype(BF16), v)
    acc_sc[rows, :] = jnp.concatenate([alpha] * (KV_LORA // LANES), axis=-1) * acc_sc[rows, :] + pv
    m_sc[rows, :] = m_new


def _attn_finish(l_sc, acc_sc, g_ref, wuv_ref, o_ref, tq):
    for hd in range(MLA_HEADS):
        rows = slice(hd * tq, (hd + 1) * tq)
        cols = slice(hd * V_DIM, (hd + 1) * V_DIM)
        inv = 1.0 / jnp.sum(l_sc[rows, :], axis=-1, keepdims=True)
        o = _dot((acc_sc[rows, :] * inv).astype(BF16), wuv_ref[hd])
        o_ref[:, cols] = (o * _silu(g_ref[:, cols])).astype(o_ref.dtype)


def _attn_init(m_sc, l_sc, acc_sc):
    m_sc[...] = jnp.full(m_sc.shape, -jnp.inf, F32)
    l_sc[...] = jnp.zeros(l_sc.shape, F32)
    acc_sc[...] = jnp.zeros(acc_sc.shape, F32)


def _attn_scratch(rows):
    return [pltpu.VMEM((rows, LANES), F32), pltpu.VMEM((rows, LANES), F32), pltpu.VMEM((rows, KV_LORA), F32)]


def _attn_prompt_kernel(q_ref, k_ref, g_ref, wuv_ref, o_ref, m_sc, l_sc, acc_sc, sa_sc, sb_sc, *, tq):
    i = pl.program_id(0)
    _attn_init(m_sc, l_sc, acc_sc)
    q = q_ref[...].reshape(MLA_HEADS * tq, QCAT)

    tk = tq // 2

    def keys(j):
        return k_ref[pl.ds(pl.multiple_of(j * tk, tk), tk), :]

    def step(s_cur, s_next, j, allowed):
        s_next[...] = _scores(q, keys(j + 1))
        _softmax_pv(s_cur[...], keys(j)[:, :KV_LORA], m_sc, l_sc, acc_sc, allowed)

    def own_mask(half):
        qchunk = lax.broadcasted_iota(jnp.int32, (tq, 1), 0) // CHUNK
        kchunk = (half * tk + lax.broadcasted_iota(jnp.int32, (1, tk), 1)) // CHUNK
        return jnp.concatenate([kchunk <= qchunk] * MLA_HEADS, axis=0)

    sa_sc[...] = _scores(q, keys(0))

    def body(jj, carry):
        step(sa_sc, sb_sc, 2 * jj, None)
        step(sb_sc, sa_sc, 2 * jj + 1, None)
        return carry

    lax.fori_loop(0, i, body, 0)
    step(sa_sc, sb_sc, 2 * i, own_mask(0))
    _softmax_pv(sb_sc[...], keys(2 * i + 1)[:, :KV_LORA], m_sc, l_sc, acc_sc, own_mask(1))
    _attn_finish(l_sc, acc_sc, g_ref, wuv_ref, o_ref, tq)


def _attn_prompt_call(qcat, kcat, g, wuv_t, tq):
    _, rows, _ = qcat.shape
    assert tq % (2 * CHUNK) == 0 and rows % tq == 0
    r8 = MLA_HEADS * tq
    return pl.pallas_call(
        functools.partial(_attn_prompt_kernel, tq=tq),
        grid=(rows // tq,),
        in_specs=[pl.BlockSpec((MLA_HEADS, tq, QCAT), lambda i: (0, i, 0)),
                  pl.BlockSpec(kcat.shape, lambda i: (0, 0), pipeline_mode=pl.Buffered(1)),
                  pl.BlockSpec((tq, MLA_WIDTH), lambda i: (i, 0)),
                  _const_spec(wuv_t.shape)],
        out_specs=pl.BlockSpec((tq, MLA_WIDTH), lambda i: (i, 0)),
        out_shape=jax.ShapeDtypeStruct((rows, MLA_WIDTH), BF16),
        scratch_shapes=_attn_scratch(r8) + [pltpu.VMEM((r8, tq // 2), F32), pltpu.VMEM((r8, tq // 2), F32)],
        compiler_params=pltpu.CompilerParams(dimension_semantics=("arbitrary",), vmem_limit_bytes=VMEM_LIMIT),
        name="attn_prompt",
    )(qcat, kcat, g, wuv_t)


def _attn_sample_kernel(q_ref, pckv_ref, pkpe_ref, knew_ref, g_ref, wuv_ref, o_ref, m_sc, l_sc, acc_sc, *, tq, tk):
    q = q_ref[...].reshape(MLA_HEADS * tq, QCAT)
    _attn_init(m_sc, l_sc, acc_sc)
    past = pckv_ref.shape[0]

    def body(j, carry):
        r0 = pl.multiple_of(j * tk, tk)
        kpe = pkpe_ref[pl.ds(r0, tk), :]
        kblk = jnp.concatenate([pckv_ref[pl.ds(r0, tk), :], kpe, kpe], axis=-1).astype(BF16)
        _softmax_pv(_scores(q, kblk), kblk[:, :KV_LORA], m_sc, l_sc, acc_sc, None)
        return carry

    lax.fori_loop(0, past // tk, body, 0)
    knew = knew_ref[...]
    _softmax_pv(_scores(q, knew), knew[:, :KV_LORA], m_sc, l_sc, acc_sc, None)
    _attn_finish(l_sc, acc_sc, g_ref, wuv_ref, o_ref, tq)


def _attn_sample_call(qcat, past_ckv, past_kpe, kcat, g, wuv_t, batch, length, tk):
    past = past_ckv.shape[1]
    assert past % CHUNK == 0 and length <= CHUNK and past % tk == 0
    rows = batch * length
    r8 = MLA_HEADS * length
    return pl.pallas_call(
        functools.partial(_attn_sample_kernel, tq=length, tk=tk),
        grid=(batch,),
        in_specs=[pl.BlockSpec((MLA_HEADS, length, QCAT), lambda b: (0, b, 0)),
                  pl.BlockSpec((None, past, KV_LORA), lambda b: (b, 0, 0)),
                  pl.BlockSpec((None, past, ROPE_DIM), lambda b: (b, 0, 0)),
                  pl.BlockSpec((length, QCAT), lambda b: (b, 0)),
                  pl.BlockSpec((length, MLA_WIDTH), lambda b: (b, 0)),
                  _const_spec(wuv_t.shape)],
        out_specs=pl.BlockSpec((length, MLA_WIDTH), lambda b: (b, 0)),
        out_shape=jax.ShapeDtypeStruct((rows, MLA_WIDTH), BF16),
        scratch_shapes=_attn_scratch(r8),
        compiler_params=pltpu.CompilerParams(dimension_semantics=("arbitrary",), vmem_limit_bytes=VMEM_LIMIT),
        name="attn_sample",
    )(qcat, past_ckv, past_kpe, kcat, g, wuv_t)


def _ssd_constants(cl):
    hb = LANES // cl
    ltri = np.tril(np.ones((cl, cl), np.float32))
    exp_s = np.zeros((LANES, SSD_HEADS * cl), np.float32)
    exp_p = np.zeros((LANES, SSD_WIDTH), np.float32)
    for hd in range(SSD_HEADS):
        exp_s[hd, hd * cl:(hd + 1) * cl] = 1.0
        exp_p[hd, hd * SSD_HEAD_DIM:(hd + 1) * SSD_HEAD_DIM] = 1.0
    t = np.arange(cl)[:, None]
    s = np.tile(np.arange(cl), SSD_HEADS)[None, :]
    eye = (t == s).astype(np.float32)
    tril = (t >= s).astype(np.float32)
    hrow = np.repeat(np.arange(hb), cl)[:, None]
    hcol = np.repeat(np.arange(hb), SSD_HEAD_DIM)[None, :]
    bdiag = (hrow == hcol).astype(np.float32)
    bf = lambda a: jnp.asarray(a, BF16)
    return (bf(np.tile(ltri, (1, CUM_PIECES))), bf(np.tile(exp_s, (CUM_PIECES, 1))),
            bf(np.tile(exp_p, (WIDE_PIECES, 1))), jnp.asarray(eye), jnp.asarray(tril), bf(bdiag))


def _split(x, n):
    pieces = []
    for _ in range(n):
        p = x.astype(BF16)
        pieces.append(p)
        x = x - p.astype(F32)
    return pieces


def _expand(x, sel, n):
    return _dot(jnp.concatenate(_split(x, n), axis=-1), sel)


def _ssd_kernel(xbc_ref, dt_ref, z_ref, cst_ref, h0_ref, cw_ref, cb_ref, dtb_ref, alog_ref, dsk_ref, nw_ref,
                ltri_ref, exps_ref, expp_ref, eye_ref, tril_ref, bdiag_ref,
                y_ref, cnew_ref, hout_ref, buf_sc, act_sc, h_sc, wide_sc, e_sc, *, cl, cps):
    step = pl.program_id(1)
    last = pl.num_programs(1) - 1
    rows = cl * cps
    keep = CONV_W - 1
    base = SUBLANES - keep

    @pl.when(step == 0)
    def _():
        buf_sc[base:SUBLANES, :] = cst_ref[...]
        h_sc[...] = h0_ref[...]

    buf_sc[SUBLANES:SUBLANES + rows, :] = xbc_ref[...]
    conv = cb_ref[...] + cw_ref[0:1, :] * buf_sc[base:base + rows, :]
    for k in range(1, CONV_W):
        conv = conv + cw_ref[k:k + 1, :] * buf_sc[base + k:base + k + rows, :]
    act_sc[...] = _silu(conv)
    tail = buf_sc[base + rows:SUBLANES + rows, :]
    buf_sc[base:SUBLANES, :] = tail

    @pl.when(step == last)
    def _():
        cnew_ref[...] = tail

    x = dt_ref[...] + dtb_ref[...]
    dt = jnp.maximum(x, 0.0) + jnp.log1p(jnp.exp(-jnp.abs(x)))
    a_pieces = _split(dt * -jnp.exp(alog_ref[...]), CUM_PIECES)
    cums, totals = [], []
    for c in range(cps):
        stacked = jnp.concatenate([p[c * cl:(c + 1) * cl, :] for p in a_pieces], axis=0)
        cum_c = _dot(ltri_ref[...], stacked)
        cums.append(cum_c)
        totals.append(jnp.broadcast_to(cum_c[cl - 1:cl, :], cum_c.shape))
    cum = jnp.concatenate(cums, axis=0)
    total = jnp.concatenate(totals, axis=0)
    wide_sc[...] = _expand(jnp.concatenate([dt, jnp.exp(total - cum), jnp.exp(cum)], axis=0), expp_ref[...],
                           WIDE_PIECES)
    e_sc[...] = _expand(cum, exps_ref[...], CUM_PIECES)

    hb = LANES // cl
    gs = GROUP_WIDTH
    gm = HEADS_PER_GROUP * cl

    def chunk(c, carry):
        r0 = pl.multiple_of(c * cl, cl)
        xs = act_sc[pl.ds(r0, cl), 0:SSD_WIDTH]
        bm = act_sc[pl.ds(r0, cl), SSD_WIDTH:SSD_WIDTH + SSD_GROUPS * SSD_STATE]
        cm = act_sc[pl.ds(r0, cl), SSD_WIDTH + SSD_GROUPS * SSD_STATE:CONV_DIM]
        dt_w = wide_sc[pl.ds(r0, cl), :]
        dstate_w = wide_sc[pl.ds(pl.multiple_of(rows + r0, cl), cl), :]
        ecum_w = wide_sc[pl.ds(pl.multiple_of(2 * rows + r0, cl), cl), :]
        e = e_sc[pl.ds(r0, cl), :]
        f_row = jnp.sum(jnp.where(eye_ref[...] > 0.0, e, 0.0), axis=0, keepdims=True)
        decay = jnp.where(tril_ref[...] > 0.0, jnp.exp(e - f_row), 0.0)
        xd = xs * dt_w
        w_state = (xd * dstate_w).astype(BF16)
        xd16 = xd.astype(BF16)
        h_prev = h_sc[...]
        h16 = h_prev.astype(BF16)
        bdiag = bdiag_ref[...]
        y_parts, st_parts = [], []
        for g in range(SSD_GROUPS):
            bg = bm[:, g * SSD_STATE:(g + 1) * SSD_STATE].astype(BF16)
            cg = cm[:, g * SSD_STATE:(g + 1) * SSD_STATE].astype(BF16)
            cb = lax.dot_general(cg, jnp.concatenate([bg] * hb, axis=0), (((1,), (1,)), ((), ())),
                                 preferred_element_type=F32)
            y_off = _dot(cg, h16[:, g * gs:(g + 1) * gs]) * ecum_w[:, g * gs:(g + 1) * gs]
            diag = []
            for j in range(gm // LANES):
                m = (cb * decay[:, g * gm + j * LANES:g * gm + (j + 1) * LANES]).astype(BF16)
                c0 = g * gs + j * hb * SSD_HEAD_DIM
                xj = xd16[:, c0:c0 + hb * SSD_HEAD_DIM]
                diag.append(_dot(m, jnp.concatenate([xj] * hb, axis=0) * bdiag))
            y_parts.append(y_off + jnp.concatenate(diag, axis=-1))
            st_parts.append(lax.dot_general(bg, w_state[:, g * gs:(g + 1) * gs], (((0,), (0,)), ((), ())),
                                            preferred_element_type=F32))
        h_sc[...] = h_prev * ecum_w[cl - 1:cl, :] + jnp.concatenate(st_parts, axis=-1)
        y = jnp.concatenate(y_parts, axis=-1) + xs * dsk_ref[...]
        y = y * _silu(z_ref[pl.ds(r0, cl), :])
        normed = [_rms(y[:, g * gs:(g + 1) * gs], nw_ref[:, g * gs:(g + 1) * gs]) for g in range(SSD_GROUPS)]
        y_ref[pl.ds(r0, cl), :] = jnp.concatenate(normed, axis=-1).astype(y_ref.dtype)
        return carry

    lax.fori_loop(0, cps, chunk, 0, unroll=True)

    @pl.when(step == last)
    def _():
        hout_ref[...] = h_sc[...]


def _ssd_call(xbc, dt_raw, z, conv_state, h0_t, w, batch, length, cps):
    cl = min(CHUNK, length)
    rows = cl * cps
    assert length % rows == 0 and LANES % cl == 0 and rows >= CONV_W - 1
    consts = _ssd_constants(cl)
    seq = lambda n: pl.BlockSpec((None, rows, n), lambda b, s: (b, s, 0))
    per_batch = lambda shape: pl.BlockSpec((None,) + shape, lambda b, s: (b, 0, 0))
    params = (w["conv_w"], w["conv_b"], w["dt_bias"], w["a_log"], w["d_skip"], w["ssd_norm_w"])
    in_specs = ([seq(CONV_DIM), seq(LANES), seq(SSD_WIDTH),
                 per_batch((CONV_W - 1, CONV_DIM)), per_batch((SSD_STATE, SSD_WIDTH))]
                + [pl.BlockSpec(a.shape, lambda b, s: (0, 0)) for a in params + consts])
    out_shape = (jax.ShapeDtypeStruct((batch, length, SSD_WIDTH), BF16),
                 jax.ShapeDtypeStruct((batch, CONV_W - 1, CONV_DIM), F32),
                 jax.ShapeDtypeStruct((batch, SSD_STATE, SSD_WIDTH), F32))
    out_specs = (seq(SSD_WIDTH), per_batch((CONV_W - 1, CONV_DIM)), per_batch((SSD_STATE, SSD_WIDTH)))
    return pl.pallas_call(
        functools.partial(_ssd_kernel, cl=cl, cps=cps),
        grid=(batch, length // rows), in_specs=in_specs, out_specs=out_specs, out_shape=out_shape,
        scratch_shapes=[pltpu.VMEM((rows + SUBLANES, CONV_DIM), F32), pltpu.VMEM((rows, CONV_DIM), F32),
                        pltpu.VMEM((SSD_STATE, SSD_WIDTH), F32), pltpu.VMEM((3 * rows, SSD_WIDTH), F32),
                        pltpu.VMEM((rows, SSD_HEADS * cl), F32)],
        compiler_params=pltpu.CompilerParams(dimension_semantics=("arbitrary", "arbitrary"),
                                             vmem_limit_bytes=VMEM_LIMIT),
        name="ssd",
    )(xbc.reshape(batch, length, CONV_DIM), dt_raw.reshape(batch, length, LANES),
      z.reshape(batch, length, SSD_WIDTH), conv_state, h0_t, *params, *consts)


def _post_kernel(mla_ref, ssd_ref, x_ref, wa_ref, wb_ref, fw_ref, o_ref, *, final):
    y = x_ref[...] + _dot(mla_ref[...], wa_ref[...]) + _dot(ssd_ref[...], wb_ref[...])
    o_ref[...] = _rms(y, fw_ref[...]) if final else y


def _post_call(mla, ssd, x, w, final_norm_w, final, tr):
    rows, d = x.shape
    row = lambda n: pl.BlockSpec((tr, n), lambda i: (i, 0))
    return pl.pallas_call(
        functools.partial(_post_kernel, final=final),
        grid=(rows // tr,),
        in_specs=[row(MLA_WIDTH), row(SSD_WIDTH), row(d), _const_spec(w["w_out_mla"].shape),
                  _const_spec(w["w_out_ssd"].shape), _const_spec(final_norm_w.shape)],
        out_specs=row(d), out_shape=jax.ShapeDtypeStruct((rows, d), F32),
        compiler_params=pltpu.CompilerParams(dimension_semantics=("arbitrary",), vmem_limit_bytes=VMEM_LIMIT),
        name="post",
    )(mla, ssd, x, w["w_out_mla"], w["w_out_ssd"], final_norm_w)


def _swap_halves(wcols):
    k, n = wcols.shape
    blocks = wcols.reshape(k, n // ROPE_DIM, 2, ROPE_DIM // 2)
    return blocks[:, :, ::-1, :].reshape(k, n)


def _layer_weights(i, norm_w, w_in, q_norm_w, w_uq, kv_norm_w, w_uk, w_uv, conv_w, conv_b, dt_bias, a_log, d_skip,
                   ssd_norm_w, w_out):
    d = w_in.shape[1]
    offs = np.cumsum((0, Q_LORA, KV_LORA, ROPE_DIM, MLA_WIDTH, SSD_WIDTH, CONV_DIM, SSD_HEADS))
    col = lambda j: w_in[i][:, offs[j]:offs[j + 1]]
    w_kpe = col(2)
    w_kpe_sw = _swap_halves(w_kpe)
    w_sm = jnp.concatenate([w_kpe, w_kpe, w_kpe_sw, w_kpe_sw], axis=1)
    w_qdt = jnp.concatenate([col(0), col(6), jnp.zeros((d, LANES - SSD_HEADS), F32)], axis=1)
    uq = w_uq[i].reshape(Q_LORA, MLA_HEADS, NOPE_DIM + ROPE_DIM)
    uq_pe = uq[:, :, NOPE_DIM:].reshape(Q_LORA, MLA_HEADS * ROPE_DIM)
    pad_heads = lambda v: jnp.concatenate([v, jnp.zeros((LANES - SSD_HEADS,), F32)])[None, :]
    return {
        "norm_w": norm_w[i][None, :],
        "w_q": w_qdt.astype(BF16), "w_kv": col(1).astype(BF16), "w_g": col(3).astype(BF16),
        "w_z": col(4).astype(BF16), "w_xbc": col(5).astype(BF16), "w_sm": w_sm.astype(BF16),
        "q_norm_w": q_norm_w[i][None, :],
        "w_uq_nope": uq[:, :, :NOPE_DIM].reshape(Q_LORA, MLA_HEADS * NOPE_DIM).astype(BF16),
        "w_uq_pe": uq_pe.astype(BF16), "w_uq_pe_sw": _swap_halves(uq_pe).astype(BF16),
        "kv_norm_w": kv_norm_w[i][None, :],
        "w_uk_t": jnp.transpose(w_uk[i], (1, 2, 0)).astype(BF16),
        "w_uv_t": jnp.transpose(w_uv[i], (1, 0, 2)).astype(BF16),
        "conv_w": conv_w[i], "conv_b": conv_b[i][None, :],
        "dt_bias": pad_heads(dt_bias[i]), "a_log": pad_heads(a_log[i]),
        "d_skip": jnp.repeat(d_skip[i], SSD_HEAD_DIM)[None, :],
        "ssd_norm_w": ssd_norm_w[i][None, :],
        "w_out_mla": w_out[i][:MLA_WIDTH].astype(BF16), "w_out_ssd": w_out[i][MLA_WIDTH:].astype(BF16),
    }


def _rope_tables(past, length, batch):
    half = ROPE_DIM // 2
    inv = 1.0 / (ROPE_THETA ** (np.arange(half, dtype=np.float64) / half))
    ang = (past + np.arange(length, dtype=np.float64))[:, None] * inv[None, :]
    cos, sin = np.cos(ang), np.sin(ang)
    reps = LANES // ROPE_DIM
    cos2 = np.tile(np.concatenate([cos, cos], axis=-1), (batch, reps))
    sin2 = np.tile(np.concatenate([-sin, sin], axis=-1), (batch, reps))
    return jnp.asarray(cos2, F32), jnp.asarray(sin2, F32)


def _state_to_lanes(h):
    b = h.shape[0]
    return jnp.transpose(h, (0, 3, 1, 2)).reshape(b, SSD_STATE, SSD_WIDTH)


def _state_from_lanes(ht):
    b = ht.shape[0]
    return jnp.transpose(ht.reshape(b, SSD_STATE, SSD_HEADS, SSD_HEAD_DIM), (0, 2, 3, 1))


def _pick(n, prefs):
    for p in prefs:
        if n % p == 0:
            return p
    return n


def _layer(x, batch, length, past_ckv, past_kpe, conv_state, ssm_state, w, tables, final_norm_w, final):
    rows = batch * length
    tr = _pick(rows, (256, 128))
    qcat, ckv, kpe, kcat, g, z, xbc, dt_raw = _pre_call(x, w, tables[0], tables[1], tr)
    if past_ckv is None:
        assert batch == 1
        tq = _pick(length, (512, 256, 128))
        mla = _attn_prompt_call(qcat, kcat, g, w["w_uv_t"], tq)
    else:
        mla = _attn_sample_call(qcat, past_ckv, past_kpe, kcat, g, w["w_uv_t"], batch, length,
                                _pick(past_ckv.shape[1], (512, 256, 128, 64)))
    cl = min(CHUNK, length)
    cps = _pick(length // cl, (4, 2, 1))
    ssd, conv_new, h_t = _ssd_call(xbc, dt_raw, z, conv_state, _state_to_lanes(ssm_state), w, batch, length, cps)
    y = _post_call(mla, ssd.reshape(rows, SSD_WIDTH), x, w, final_norm_w, final, tr)
    return (y, ckv.reshape(batch, length, KV_LORA), kpe.reshape(batch, length, ROPE_DIM), conv_new,
            _state_from_lanes(h_t))


def kernel(x_prompt, x_sample, cache_ckv, cache_kpe, state_conv, state_ssm, norm_w, w_in, q_norm_w, w_uq,
           kv_norm_w, w_uk, w_uv, conv_w, conv_b, dt_bias, a_log, d_skip, ssd_norm_w, w_out, final_norm_w):
    depth = w_in.shape[0]
    bp, lp, d = x_prompt.shape
    bs, ls, _ = x_sample.shape
    past = cache_ckv.shape[2]
    tab_p = _rope_tables(0, lp, bp)
    tab_s = _rope_tables(past, ls, bs)
    fw = final_norm_w[None, :]
    yp, ys = x_prompt.reshape(bp * lp, d), x_sample.reshape(bs * ls, d)
    outs_p, outs_s = [], []
    for i in range(depth):
        w = _layer_weights(i, norm_w, w_in, q_norm_w, w_uq, kv_norm_w, w_uk, w_uv, conv_w, conv_b, dt_bias, a_log,
                           d_skip, ssd_norm_w, w_out)
        final = i == depth - 1
        yp, *new_p = _layer(yp, bp, lp, None, None, jnp.zeros((bp, CONV_W - 1, CONV_DIM), F32),
                            jnp.zeros((bp, SSD_HEADS, SSD_HEAD_DIM, SSD_STATE), F32), w, tab_p, fw, final)
        ys, *new_s = _layer(ys, bs, ls, cache_ckv[i], cache_kpe[i], state_conv[i], state_ssm[i], w, tab_s, fw, final)
        outs_p.append(new_p)
        outs_s.append(new_s)
    stack = lambda outs, j: jnp.stack([o[j] for o in outs])
    return (yp.reshape(bp, lp, d), ys.reshape(bs, ls, d),
            stack(outs_p, 0), stack(outs_p, 1), stack(outs_p, 2), stack(outs_p, 3),
            stack(outs_s, 0), stack(outs_s, 1), stack(outs_s, 2), stack(outs_s, 3))
```

```python
import functools

import numpy as np
import jax
import jax.numpy as jnp
from jax import lax
from jax.experimental import pallas as pl
from jax.experimental.pallas import tpu as pltpu

F32 = jnp.float32
BF16 = jnp.bfloat16

EPS = 1e-6
CHUNK = 64
MLA_HEADS = 8
Q_LORA = 384
KV_LORA = 256
NOPE_DIM = 128
ROPE_DIM = 64
V_DIM = 128
ROPE_THETA = 10000.0
ATTN_SCALE = (NOPE_DIM + ROPE_DIM) ** -0.5
Q_SCALE = ATTN_SCALE * float(np.log2(np.e))
SSD_HEADS = 16
SSD_HEAD_DIM = 64
SSD_GROUPS = 2
SSD_STATE = 128
CONV_W = 4
SSD_WIDTH = SSD_HEADS * SSD_HEAD_DIM
GROUP_WIDTH = SSD_WIDTH // SSD_GROUPS
HEADS_PER_GROUP = SSD_HEADS // SSD_GROUPS
CONV_DIM = SSD_WIDTH + 2 * SSD_GROUPS * SSD_STATE
MLA_WIDTH = MLA_HEADS * V_DIM
QCAT = KV_LORA + 2 * ROPE_DIM

CUM_PIECES = 3
WIDE_PIECES = 2

LANES = 128
SUBLANES = 8
VMEM_LIMIT = 56 * 1024 * 1024


def _dot(a, b):
    return jnp.dot(a, b, preferred_element_type=F32)


def _rms(x, w):
    return x * lax.rsqrt(jnp.mean(x * x, axis=-1, keepdims=True) + EPS) * w


def _silu(x):
    return x * (1.0 / (1.0 + jnp.exp(-x)))


def _const_spec(shape):
    nd = len(shape)
    return pl.BlockSpec(shape, lambda *_: (0,) * nd)


def _pre_kernel(*refs, seq_rows, tiles_per_seq, n_alias):
    (x_ref, cst_ref, nw_ref, wq_ref, wkv_ref, wg_ref, wz_ref, wxbc_ref, wsm_ref, qnw_ref, wuqn_ref, wuqp_ref,
     wuqps_ref, kvnw_ref, wuk_ref, cw_ref, cb_ref, dtb_ref, cos_ref, sin_ref,
     qcat_ref, ckv_ref, kpe_ref, kcat_ref, g_ref, z_ref, xbc_ref, dt_ref, cnew_ref, buf_sc) = refs[n_alias:]
    keep = CONV_W - 1
    base = SUBLANES - keep
    if tiles_per_seq > 1:
        @pl.when(pl.program_id(0) % tiles_per_seq == 0)
        def _():
            buf_sc[base:SUBLANES, :] = cst_ref[0]

    x = x_ref[...]
    h = _rms(x, nw_ref[...]).astype(BF16)
    xbc = _dot(h, wxbc_ref[...])
    g_ref[...] = _silu(_dot(h, wg_ref[...]))
    z_ref[...] = _silu(_dot(h, wz_ref[...]))

    cos2 = cos_ref[...]
    sin2 = sin_ref[...]
    sm = _dot(h, wsm_ref[...])
    kpe2 = sm[:, 0:LANES] * cos2 + sm[:, LANES:2 * LANES] * sin2
    kpe_ref[...] = kpe2[:, :ROPE_DIM]
    ckv = _rms(_dot(h, wkv_ref[...]), kvnw_ref[...])
    ckv_ref[...] = ckv
    kcat_ref[...] = jnp.concatenate([ckv, kpe2], axis=-1).astype(BF16)

    qdt = _dot(h, wq_ref[...])
    dt_in = qdt[:, Q_LORA:Q_LORA + LANES] + dtb_ref[...]
    dt_ref[...] = jnp.maximum(dt_in, 0.0) + jnp.log1p(jnp.exp(-jnp.abs(dt_in)))
    qn = _rms(qdt[:, :Q_LORA], qnw_ref[...]).astype(BF16)
    qnope = _dot(qn, wuqn_ref[...])
    pairs = MLA_HEADS // 2
    cos_all = jnp.concatenate([cos2] * pairs, axis=-1)
    sin_all = jnp.concatenate([sin2] * pairs, axis=-1)
    qpe = _dot(qn, wuqp_ref[...]) * cos_all + _dot(qn, wuqps_ref[...]) * sin_all
    lane = lax.broadcasted_iota(jnp.int32, (x.shape[0], LANES), 1)
    for hd in range(MLA_HEADS):
        qlat = _dot(qnope[:, hd * NOPE_DIM:(hd + 1) * NOPE_DIM].astype(BF16), wuk_ref[hd]) * Q_SCALE
        pair = qpe[:, (hd // 2) * LANES:(hd // 2 + 1) * LANES] * Q_SCALE
        own_half = (lane < ROPE_DIM) if hd % 2 == 0 else (lane >= ROPE_DIM)
        qcat_ref[hd] = jnp.concatenate([qlat, jnp.where(own_half, pair, 0.0)], axis=-1).astype(BF16)

    for s in range(x.shape[0] // seq_rows):
        if tiles_per_seq == 1:
            buf_sc[base:SUBLANES, :] = cst_ref[s]
        buf_sc[SUBLANES:SUBLANES + seq_rows, :] = xbc[s * seq_rows:(s + 1) * seq_rows, :]
        conv = cb_ref[...] + cw_ref[0:1, :] * buf_sc[base:base + seq_rows, :]
        for k in range(1, CONV_W):
            conv = conv + cw_ref[k:k + 1, :] * buf_sc[base + k:base + k + seq_rows, :]
        xbc_ref[s * seq_rows:(s + 1) * seq_rows, :] = _silu(conv)
        tail = buf_sc[base + seq_rows:SUBLANES + seq_rows, :]
        buf_sc[base:SUBLANES, :] = tail
        cnew_ref[s] = tail


def _pre_call(x, conv_state, layer, state_layer, stacked, w, cos2, sin2, batch, length, tr):
    rows, d = x.shape
    depth = w["depth"]
    seq_rows = min(length, tr)
    tiles_per_seq = length // seq_rows
    nseq = tr // seq_rows
    assert rows == batch * length and length % seq_rows == 0 and tr % seq_rows == 0 and seq_rows >= CONV_W - 1
    row = lambda n: pl.BlockSpec((tr, n), lambda i: (i, 0))
    layer_row = lambda n: pl.BlockSpec((None, tr, n), lambda i: (layer, i, 0))
    weights = (w["norm_w"], w["w_q"], w["w_kv"], w["w_g"], w["w_z"], w["w_xbc"], w["w_sm"],
               w["q_norm_w"], w["w_uq_nope"], w["w_uq_pe"], w["w_uq_pe_sw"], w["kv_norm_w"], w["w_uk_t"],
               w["conv_w"], w["conv_b"], w["dt_bias"])
    aliased = () if stacked is None else tuple(stacked)
    in_specs = ([pl.BlockSpec(memory_space=pl.ANY)] * len(aliased)
                + [row(d), pl.BlockSpec((None, nseq, CONV_W - 1, CONV_DIM),
                                        lambda i: (state_layer, i // tiles_per_seq, 0, 0))]
                + [_const_spec(a.shape) for a in weights] + [row(LANES), row(LANES)])
    out_shape = (
        jax.ShapeDtypeStruct((MLA_HEADS, rows, QCAT), BF16),
        jax.ShapeDtypeStruct((depth, rows, KV_LORA), F32),
        jax.ShapeDtypeStruct((depth, rows, ROPE_DIM), F32),
        jax.ShapeDtypeStruct((rows, QCAT), BF16),
        jax.ShapeDtypeStruct((rows, MLA_WIDTH), F32),
        jax.ShapeDtypeStruct((rows, SSD_WIDTH), F32),
        jax.ShapeDtypeStruct((rows, CONV_DIM), F32),
        jax.ShapeDtypeStruct((rows, LANES), F32),
        jax.ShapeDtypeStruct((batch, CONV_W - 1, CONV_DIM), F32),
    )
    out_specs = (
        pl.BlockSpec((MLA_HEADS, tr, QCAT), lambda i: (0, i, 0)),
        layer_row(KV_LORA), layer_row(ROPE_DIM), row(QCAT), row(MLA_WIDTH), row(SSD_WIDTH), row(CONV_DIM),
        row(LANES), pl.BlockSpec((nseq, CONV_W - 1, CONV_DIM), lambda i: (i // tiles_per_seq, 0, 0)),
    )
    return pl.pallas_call(
        functools.partial(_pre_kernel, seq_rows=seq_rows, tiles_per_seq=tiles_per_seq, n_alias=len(aliased)),
        grid=(rows // tr,), in_specs=in_specs, out_specs=out_specs, out_shape=out_shape,
        scratch_shapes=[pltpu.VMEM((seq_rows + SUBLANES, CONV_DIM), F32)],
        input_output_aliases={k: 1 + k for k in range(len(aliased))},
        compiler_params=pltpu.CompilerParams(dimension_semantics=("arbitrary",), vmem_limit_bytes=VMEM_LIMIT),
        name="pre",
    )(*aliased, x, conv_state, *weights, cos2, sin2)


def _scores(q, kblk):
    return lax.dot_general(q, kblk, (((1,), (1,)), ((), ())), preferred_element_type=F32)


def _softmax_pv(s, v, m_sc, l_sc, acc_sc, allowed):
    rows = slice(0, s.shape[0])
    if allowed is not None:
        s = jnp.where(allowed, s, -jnp.inf)
    tk = s.shape[1]
    m_prev = m_sc[rows, :]
    if tk % LANES == 0:
        chunks = [s[:, c * LANES:(c + 1) * LANES] for c in range(tk // LANES)]
        m_new = jnp.maximum(m_prev, jnp.max(functools.reduce(jnp.maximum, chunks), axis=-1, keepdims=True))
        ps = [jnp.exp2(c - m_new) for c in chunks]
        psum = functools.reduce(jnp.add, ps)
        p = jnp.concatenate(ps, axis=-1)
    else:
        m_new = jnp.maximum(m_prev, jnp.max(s, axis=-1, keepdims=True))
        p = jnp.exp2(s - m_new[:, :1])
        psum = jnp.sum(p, axis=-1, keepdims=True) * (1.0 / LANES)
    alpha = jnp.exp2(m_prev - m_new)
    l_sc[rows, :] = alpha * l_sc[rows, :] + psum
    pv = _dot(p.astype(BF16), v)
    acc_sc[rows, :] = jnp.concatenate([alpha] * (KV_LORA // LANES), axis=-1) * acc_sc[rows, :] + pv
    m_sc[rows, :] = m_new


def _attn_finish(l_sc, acc_sc, g_ref, wuv_ref, o_ref, tq):
    for hd in range(MLA_HEADS):
        rows = slice(hd * tq, (hd + 1) * tq)
        cols = slice(hd * V_DIM, (hd + 1) * V_DIM)
        inv = 1.0 / jnp.sum(l_sc[rows, :], axis=-1, keepdims=True)
        o = _dot((acc_sc[rows, :] * inv).astype(BF16), wuv_ref[hd])
        o_ref[:, cols] = (o * g_ref[:, cols]).astype(o_ref.dtype)


def _attn_init(m_sc, l_sc, acc_sc):
    m_sc[...] = jnp.full(m_sc.shape, -jnp.inf, F32)
    l_sc[...] = jnp.zeros(l_sc.shape, F32)
    acc_sc[...] = jnp.zeros(acc_sc.shape, F32)


def _attn_scratch(rows):
    return [pltpu.VMEM((rows, LANES), F32), pltpu.VMEM((rows, LANES), F32), pltpu.VMEM((rows, KV_LORA), F32)]


def _attn_prompt_kernel(q_ref, k_ref, g_ref, wuv_ref, o_ref, m_sc, l_sc, acc_sc, sa_sc, sb_sc, *, tq):
    i = pl.program_id(0)
    _attn_init(m_sc, l_sc, acc_sc)
    q = q_ref[...].reshape(MLA_HEADS * tq, QCAT)

    tk = tq // 2

    def keys(j):
        return k_ref[pl.ds(pl.multiple_of(j * tk, tk), tk), :]

    def step(s_cur, s_next, j, allowed):
        s_next[...] = _scores(q, keys(j + 1))
        _softmax_pv(s_cur[...], keys(j)[:, :KV_LORA], m_sc, l_sc, acc_sc, allowed)

    def own_mask(half):
        qchunk = lax.broadcasted_iota(jnp.int32, (tq, 1), 0) // CHUNK
        kchunk = (half * tk + lax.broadcasted_iota(jnp.int32, (1, tk), 1)) // CHUNK
        return jnp.concatenate([kchunk <= qchunk] * MLA_HEADS, axis=0)

    sa_sc[...] = _scores(q, keys(0))

    def body(jj, carry):
        step(sa_sc, sb_sc, 2 * jj, None)
        step(sb_sc, sa_sc, 2 * jj + 1, None)
        return carry

    lax.fori_loop(0, i, body, 0)
    step(sa_sc, sb_sc, 2 * i, own_mask(0))
    _softmax_pv(sb_sc[...], keys(2 * i + 1)[:, :KV_LORA], m_sc, l_sc, acc_sc, own_mask(1))
    _attn_finish(l_sc, acc_sc, g_ref, wuv_ref, o_ref, tq)


def _attn_prompt_call(qcat, kcat, g, wuv_t, tq):
    _, rows, _ = qcat.shape
    assert tq % (2 * CHUNK) == 0 and rows % tq == 0
    r8 = MLA_HEADS * tq
    return pl.pallas_call(
        functools.partial(_attn_prompt_kernel, tq=tq),
        grid=(rows // tq,),
        in_specs=[pl.BlockSpec((MLA_HEADS, tq, QCAT), lambda i: (0, i, 0)),
                  pl.BlockSpec(kcat.shape, lambda i: (0, 0), pipeline_mode=pl.Buffered(1)),
                  pl.BlockSpec((tq, MLA_WIDTH), lambda i: (i, 0)),
                  _const_spec(wuv_t.shape)],
        out_specs=pl.BlockSpec((tq, MLA_WIDTH), lambda i: (i, 0)),
        out_shape=jax.ShapeDtypeStruct((rows, MLA_WIDTH), BF16),
        scratch_shapes=_attn_scratch(r8) + [pltpu.VMEM((r8, tq // 2), F32), pltpu.VMEM((r8, tq // 2), F32)],
        compiler_params=pltpu.CompilerParams(dimension_semantics=("arbitrary",), vmem_limit_bytes=VMEM_LIMIT),
        name="attn_prompt",
    )(qcat, kcat, g, wuv_t)


def _attn_sample_kernel(q_ref, pckv_ref, pkpe_ref, knew_ref, g_ref, wuv_ref, o_ref, m_sc, l_sc, acc_sc, *, tq, tk):
    q = q_ref[...].reshape(MLA_HEADS * tq, QCAT)
    _attn_init(m_sc, l_sc, acc_sc)
    past = pckv_ref.shape[0]

    def body(j, carry):
        r0 = pl.multiple_of(j * tk, tk)
        kpe = pkpe_ref[pl.ds(r0, tk), :]
        kblk = jnp.concatenate([pckv_ref[pl.ds(r0, tk), :], kpe, kpe], axis=-1).astype(BF16)
        _softmax_pv(_scores(q, kblk), kblk[:, :KV_LORA], m_sc, l_sc, acc_sc, None)
        return carry

    lax.fori_loop(0, past // tk, body, 0)
    knew = knew_ref[...]
    _softmax_pv(_scores(q, knew), knew[:, :KV_LORA], m_sc, l_sc, acc_sc, None)
    _attn_finish(l_sc, acc_sc, g_ref, wuv_ref, o_ref, tq)


def _attn_sample_call(qcat, past_ckv, past_kpe, layer, kcat, g, wuv_t, batch, length, tk):
    past = past_ckv.shape[2]
    assert past % CHUNK == 0 and length <= CHUNK and past % tk == 0
    rows = batch * length
    r8 = MLA_HEADS * length
    return pl.pallas_call(
        functools.partial(_attn_sample_kernel, tq=length, tk=tk),
        grid=(batch,),
        in_specs=[pl.BlockSpec((MLA_HEADS, length, QCAT), lambda b: (0, b, 0)),
                  pl.BlockSpec((None, None, past, KV_LORA), lambda b: (layer, b, 0, 0)),
                  pl.BlockSpec((None, None, past, ROPE_DIM), lambda b: (layer, b, 0, 0)),
                  pl.BlockSpec((length, QCAT), lambda b: (b, 0)),
                  pl.BlockSpec((length, MLA_WIDTH), lambda b: (b, 0)),
                  _const_spec(wuv_t.shape)],
        out_specs=pl.BlockSpec((length, MLA_WIDTH), lambda b: (b, 0)),
        out_shape=jax.ShapeDtypeStruct((rows, MLA_WIDTH), BF16),
        scratch_shapes=_attn_scratch(r8),
        compiler_params=pltpu.CompilerParams(dimension_semantics=("arbitrary",), vmem_limit_bytes=VMEM_LIMIT),
        name="attn_sample",
    )(qcat, past_ckv, past_kpe, kcat, g, wuv_t)


def _ssd_constants(cl):
    hb = LANES // cl
    ltri = np.tril(np.ones((cl, cl), np.float32))
    exp_s = np.zeros((LANES, SSD_HEADS * cl), np.float32)
    exp_p = np.zeros((LANES, SSD_WIDTH), np.float32)
    for hd in range(SSD_HEADS):
        exp_s[hd, hd * cl:(hd + 1) * cl] = 1.0
        exp_p[hd, hd * SSD_HEAD_DIM:(hd + 1) * SSD_HEAD_DIM] = 1.0
    t = np.arange(cl)[:, None]
    s = np.tile(np.arange(cl), SSD_HEADS)[None, :]
    eye = (t == s).astype(np.float32)
    tril = (t >= s).astype(np.float32)
    hrow = np.repeat(np.arange(hb), cl)[:, None]
    hcol = np.repeat(np.arange(hb), SSD_HEAD_DIM)[None, :]
    bdiag = (hrow == hcol).astype(np.float32)
    bf = lambda a: jnp.asarray(a, BF16)
    return (bf(np.tile(ltri, (1, CUM_PIECES))), bf(np.tile(exp_s, (CUM_PIECES, 1))),
            bf(np.tile(exp_p, (WIDE_PIECES, 1))), jnp.asarray(eye), jnp.asarray(tril), bf(bdiag))


def _split(x, n):
    pieces = []
    for _ in range(n):
        p = x.astype(BF16)
        pieces.append(p)
        x = x - p.astype(F32)
    return pieces


def _expand(x, sel, n):
    return _dot(jnp.concatenate(_split(x, n), axis=-1), sel)


def _ssd_kernel(act_ref, dt_ref, z_ref, h0_ref, alog_ref, dsk_ref, nw_ref,
                ltri_ref, exps_ref, expp_ref, eye_ref, tril_ref, bdiag_ref,
                y_ref, hout_ref, h_sc, wide_sc, e_sc, *, cl, cps):
    step = pl.program_id(1)
    last = pl.num_programs(1) - 1
    rows = cl * cps

    @pl.when(step == 0)
    def _():
        h_sc[...] = h0_ref[...]

    dt = dt_ref[...]
    a_pieces = _split(dt * -jnp.exp(alog_ref[...]), CUM_PIECES)
    cums, totals = [], []
    for c in range(cps):
        stacked = jnp.concatenate([p[c * cl:(c + 1) * cl, :] for p in a_pieces], axis=0)
        cum_c = _dot(ltri_ref[...], stacked)
        cums.append(cum_c)
        totals.append(jnp.broadcast_to(cum_c[cl - 1:cl, :], cum_c.shape))
    cum = jnp.concatenate(cums, axis=0)
    total = jnp.concatenate(totals, axis=0)
    wide_sc[...] = _expand(jnp.concatenate([dt, jnp.exp(total - cum), jnp.exp(cum)], axis=0), expp_ref[...],
                           WIDE_PIECES)
    e_sc[...] = _expand(cum, exps_ref[...], CUM_PIECES)

    hb = LANES // cl
    gs = GROUP_WIDTH
    gm = HEADS_PER_GROUP * cl

    def chunk(c, carry):
        r0 = pl.multiple_of(c * cl, cl)
        xs = act_ref[pl.ds(r0, cl), 0:SSD_WIDTH]
        bm = act_ref[pl.ds(r0, cl), SSD_WIDTH:SSD_WIDTH + SSD_GROUPS * SSD_STATE]
        cm = act_ref[pl.ds(r0, cl), SSD_WIDTH + SSD_GROUPS * SSD_STATE:CONV_DIM]
        dt_w = wide_sc[pl.ds(r0, cl), :]
        dstate_w = wide_sc[pl.ds(pl.multiple_of(rows + r0, cl), cl), :]
        ecum_w = wide_sc[pl.ds(pl.multiple_of(2 * rows + r0, cl), cl), :]
        e = e_sc[pl.ds(r0, cl), :]
        f_row = jnp.sum(jnp.where(eye_ref[...] > 0.0, e, 0.0), axis=0, keepdims=True)
        decay = jnp.where(tril_ref[...] > 0.0, jnp.exp(e - f_row), 0.0)
        xd = xs * dt_w
        w_state = (xd * dstate_w).astype(BF16)
        xd16 = xd.astype(BF16)
        h_prev = h_sc[...]
        h16 = h_prev.astype(BF16)
        bdiag = bdiag_ref[...]
        y_parts, st_parts = [], []
        for g in range(SSD_GROUPS):
            bg = bm[:, g * SSD_STATE:(g + 1) * SSD_STATE].astype(BF16)
            cg = cm[:, g * SSD_STATE:(g + 1) * SSD_STATE].astype(BF16)
            cb = lax.dot_general(cg, jnp.concatenate([bg] * hb, axis=0), (((1,), (1,)), ((), ())),
                                 preferred_element_type=F32)
            y_off = _dot(cg, h16[:, g * gs:(g + 1) * gs]) * ecum_w[:, g * gs:(g + 1) * gs]
            diag = []
            for j in range(gm // LANES):
                m = (cb * decay[:, g * gm + j * LANES:g * gm + (j + 1) * LANES]).astype(BF16)
                c0 = g * gs + j * hb * SSD_HEAD_DIM
                xj = xd16[:, c0:c0 + hb * SSD_HEAD_DIM]
                diag.append(_dot(m, jnp.concatenate([xj] * hb, axis=0) * bdiag))
            y_parts.append(y_off + jnp.concatenate(diag, axis=-1))
            st_parts.append(lax.dot_general(bg, w_state[:, g * gs:(g + 1) * gs], (((0,), (0,)), ((), ())),
                                            preferred_element_type=F32))
        h_sc[...] = h_prev * ecum_w[cl - 1:cl, :] + jnp.concatenate(st_parts, axis=-1)
        y = jnp.concatenate(y_parts, axis=-1) + xs * dsk_ref[...]
        y = y * z_ref[pl.ds(r0, cl), :]
        normed = [_rms(y[:, g * gs:(g + 1) * gs], nw_ref[:, g * gs:(g + 1) * gs]) for g in range(SSD_GROUPS)]
        y_ref[pl.ds(r0, cl), :] = jnp.concatenate(normed, axis=-1).astype(y_ref.dtype)
        return carry

    lax.fori_loop(0, cps, chunk, 0, unroll=True)

    @pl.when(step == last)
    def _():
        hout_ref[...] = h_sc[...]


def _ssd_call(act, dt, z, h0_t, state_layer, w, batch, length, cps):
    cl = min(CHUNK, length)
    rows = cl * cps
    assert length % rows == 0 and LANES % cl == 0
    consts = _ssd_constants(cl)
    seq = lambda n: pl.BlockSpec((None, rows, n), lambda b, s: (b, s, 0))
    params = (w["a_log"], w["d_skip"], w["ssd_norm_w"])
    in_specs = ([seq(CONV_DIM), seq(LANES), seq(SSD_WIDTH),
                 pl.BlockSpec((None, None, SSD_STATE, SSD_WIDTH), lambda b, s: (state_layer, b, 0, 0))]
                + [pl.BlockSpec(a.shape, lambda b, s: (0, 0)) for a in params + consts])
    out_shape = (jax.ShapeDtypeStruct((batch, length, SSD_WIDTH), BF16),
                 jax.ShapeDtypeStruct((batch, SSD_STATE, SSD_WIDTH), F32))
    out_specs = (seq(SSD_WIDTH), pl.BlockSpec((None, SSD_STATE, SSD_WIDTH), lambda b, s: (b, 0, 0)))
    return pl.pallas_call(
        functools.partial(_ssd_kernel, cl=cl, cps=cps),
        grid=(batch, length // rows), in_specs=in_specs, out_specs=out_specs, out_shape=out_shape,
        scratch_shapes=[pltpu.VMEM((SSD_STATE, SSD_WIDTH), F32), pltpu.VMEM((3 * rows, SSD_WIDTH), F32),
                        pltpu.VMEM((rows, SSD_HEADS * cl), F32)],
        compiler_params=pltpu.CompilerParams(dimension_semantics=("arbitrary", "arbitrary"),
                                             vmem_limit_bytes=VMEM_LIMIT),
        name="ssd",
    )(act.reshape(batch, length, CONV_DIM), dt.reshape(batch, length, LANES),
      z.reshape(batch, length, SSD_WIDTH), h0_t, *params, *consts)


def _post_kernel(mla_ref, ssd_ref, x_ref, wa_ref, wb_ref, fw_ref, o_ref, *, final):
    y = x_ref[...] + _dot(mla_ref[...], wa_ref[...]) + _dot(ssd_ref[...], wb_ref[...])
    o_ref[...] = _rms(y, fw_ref[...]) if final else y


def _post_call(mla, ssd, x, w, final_norm_w, final, tr):
    rows, d = x.shape
    row = lambda n: pl.BlockSpec((tr, n), lambda i: (i, 0))
    return pl.pallas_call(
        functools.partial(_post_kernel, final=final),
        grid=(rows // tr,),
        in_specs=[row(MLA_WIDTH), row(SSD_WIDTH), row(d), _const_spec(w["w_out_mla"].shape),
                  _const_spec(w["w_out_ssd"].shape), _const_spec(final_norm_w.shape)],
        out_specs=row(d), out_shape=jax.ShapeDtypeStruct((rows, d), F32),
        compiler_params=pltpu.CompilerParams(dimension_semantics=("arbitrary",), vmem_limit_bytes=VMEM_LIMIT),
        name="post",
    )(mla, ssd, x, w["w_out_mla"], w["w_out_ssd"], final_norm_w)


def _swap_halves(wcols):
    k, n = wcols.shape
    blocks = wcols.reshape(k, n // ROPE_DIM, 2, ROPE_DIM // 2)
    return blocks[:, :, ::-1, :].reshape(k, n)


def _layer_weights(i, norm_w, w_in, q_norm_w, w_uq, kv_norm_w, w_uk, w_uv, conv_w, conv_b, dt_bias, a_log, d_skip,
                   ssd_norm_w, w_out):
    d = w_in.shape[1]
    offs = np.cumsum((0, Q_LORA, KV_LORA, ROPE_DIM, MLA_WIDTH, SSD_WIDTH, CONV_DIM, SSD_HEADS))
    col = lambda j: w_in[i][:, offs[j]:offs[j + 1]]
    w_kpe = col(2)
    w_kpe_sw = _swap_halves(w_kpe)
    w_sm = jnp.concatenate([w_kpe, w_kpe, w_kpe_sw, w_kpe_sw], axis=1)
    w_qdt = jnp.concatenate([col(0), col(6), jnp.zeros((d, LANES - SSD_HEADS), F32)], axis=1)
    uq = w_uq[i].reshape(Q_LORA, MLA_HEADS, NOPE_DIM + ROPE_DIM)
    uq_pe = uq[:, :, NOPE_DIM:].reshape(Q_LORA, MLA_HEADS * ROPE_DIM)
    pad_heads = lambda v: jnp.concatenate([v, jnp.zeros((LANES - SSD_HEADS,), F32)])[None, :]
    return {
        "depth": w_in.shape[0],
        "norm_w": norm_w[i][None, :],
        "w_q": w_qdt.astype(BF16), "w_kv": col(1).astype(BF16), "w_g": col(3).astype(BF16),
        "w_z": col(4).astype(BF16), "w_xbc": col(5).astype(BF16), "w_sm": w_sm.astype(BF16),
        "q_norm_w": q_norm_w[i][None, :],
        "w_uq_nope": uq[:, :, :NOPE_DIM].reshape(Q_LORA, MLA_HEADS * NOPE_DIM).astype(BF16),
        "w_uq_pe": uq_pe.astype(BF16), "w_uq_pe_sw": _swap_halves(uq_pe).astype(BF16),
        "kv_norm_w": kv_norm_w[i][None, :],
        "w_uk_t": jnp.transpose(w_uk[i], (1, 2, 0)).astype(BF16),
        "w_uv_t": jnp.transpose(w_uv[i], (1, 0, 2)).astype(BF16),
        "conv_w": conv_w[i], "conv_b": conv_b[i][None, :],
        "dt_bias": pad_heads(dt_bias[i]), "a_log": pad_heads(a_log[i]),
        "d_skip": jnp.repeat(d_skip[i], SSD_HEAD_DIM)[None, :],
        "ssd_norm_w": ssd_norm_w[i][None, :],
        "w_out_mla": w_out[i][:MLA_WIDTH].astype(BF16), "w_out_ssd": w_out[i][MLA_WIDTH:].astype(BF16),
    }


def _rope_tables(past, length, batch):
    half = ROPE_DIM // 2
    inv = 1.0 / (ROPE_THETA ** (np.arange(half, dtype=np.float64) / half))
    ang = (past + np.arange(length, dtype=np.float64))[:, None] * inv[None, :]
    cos, sin = np.cos(ang), np.sin(ang)
    reps = LANES // ROPE_DIM
    cos2 = np.tile(np.concatenate([cos, cos], axis=-1), (batch, reps))
    sin2 = np.tile(np.concatenate([-sin, sin], axis=-1), (batch, reps))
    return jnp.asarray(cos2, F32), jnp.asarray(sin2, F32)


def _state_to_lanes(h):
    b = h.shape[0]
    return jnp.transpose(h, (0, 3, 1, 2)).reshape(b, SSD_STATE, SSD_WIDTH)


def _state_from_lanes(ht):
    b = ht.shape[0]
    return jnp.transpose(ht.reshape(b, SSD_STATE, SSD_HEADS, SSD_HEAD_DIM), (0, 2, 3, 1))


def _pick(n, prefs):
    for p in prefs:
        if n % p == 0:
            return p
    return n


def _layer(x, batch, length, layer, state_layer, caches, conv_state, ssm_state_t, stacked, w, tables, final_norm_w):
    rows = batch * length
    tr = _pick(rows, (256, 128))
    qcat, ckv, kpe, kcat, g, z, act, dt, conv_new = _pre_call(x, conv_state, layer, state_layer, stacked, w,
                                                              tables[0], tables[1], batch, length, tr)
    if caches is None:
        assert batch == 1
        mla = _attn_prompt_call(qcat, kcat, g, w["w_uv_t"], _pick(length, (512, 256, 128)))
    else:
        mla = _attn_sample_call(qcat, caches[0], caches[1], layer, kcat, g, w["w_uv_t"], batch, length,
                                _pick(caches[0].shape[2], (2048, 1024, 512, 256, 128, 64)))
    cl = min(CHUNK, length)
    ssd, h_t = _ssd_call(act, dt, z, ssm_state_t, state_layer, w, batch, length, _pick(length // cl, (4, 2, 1)))
    y = _post_call(mla, ssd.reshape(rows, SSD_WIDTH), x, w, final_norm_w, layer == w["depth"] - 1, tr)
    return y, (ckv, kpe), conv_new, h_t


def kernel(x_prompt, x_sample, cache_ckv, cache_kpe, state_conv, state_ssm, norm_w, w_in, q_norm_w, w_uq,
           kv_norm_w, w_uk, w_uv, conv_w, conv_b, dt_bias, a_log, d_skip, ssd_norm_w, w_out, final_norm_w):
    depth = w_in.shape[0]
    bp, lp, d = x_prompt.shape
    bs, ls, _ = x_sample.shape
    past = cache_ckv.shape[2]
    tab_p = _rope_tables(0, lp, bp)
    tab_s = _rope_tables(past, ls, bs)
    fw = final_norm_w[None, :]
    yp, ys = x_prompt.reshape(bp * lp, d), x_sample.reshape(bs * ls, d)
    zero_conv = jnp.zeros((1, bp, CONV_W - 1, CONV_DIM), F32)
    zero_ssm = jnp.zeros((1, bp, SSD_STATE, SSD_WIDTH), F32)
    ssm_t = _state_to_lanes(state_ssm.reshape((depth * bs,) + state_ssm.shape[2:]))
    ssm_t = ssm_t.reshape(depth, bs, SSD_STATE, SSD_WIDTH)
    kv_p = kv_s = None
    conv_p, conv_s, ssm_p, ssm_s = [], [], [], []
    for i in range(depth):
        w = _layer_weights(i, norm_w, w_in, q_norm_w, w_uq, kv_norm_w, w_uk, w_uv, conv_w, conv_b, dt_bias, a_log,
                           d_skip, ssd_norm_w, w_out)
        yp, kv_p, c_new, h_new = _layer(yp, bp, lp, i, 0, None, zero_conv, zero_ssm, kv_p, w, tab_p, fw)
        conv_p.append(c_new)
        ssm_p.append(_state_from_lanes(h_new))
        ys, kv_s, c_new, h_new = _layer(ys, bs, ls, i, i, (cache_ckv, cache_kpe), state_conv, ssm_t, kv_s, w,
                                        tab_s, fw)
        conv_s.append(c_new)
        ssm_s.append(_state_from_lanes(h_new))
    return (yp.reshape(bp, lp, d), ys.reshape(bs, ls, d),
            kv_p[0].reshape(depth, bp, lp, KV_LORA), kv_p[1].reshape(depth, bp, lp, ROPE_DIM),
            jnp.stack(conv_p), jnp.stack(ssm_p),
            kv_s[0].reshape(depth, bs, ls, KV_LORA), kv_s[1].reshape(depth, bs, ls, ROPE_DIM),
            jnp.stack(conv_s), jnp.stack(ssm_s))
```

```python
import functools

import numpy as np
import jax
import jax.numpy as jnp
from jax import lax
from jax.experimental import pallas as pl
from jax.experimental.pallas import tpu as pltpu

F32 = jnp.float32
BF16 = jnp.bfloat16

EPS = 1e-6
CHUNK = 64
MLA_HEADS = 8
Q_LORA = 384
KV_LORA = 256
NOPE_DIM = 128
ROPE_DIM = 64
V_DIM = 128
ROPE_THETA = 10000.0
ATTN_SCALE = (NOPE_DIM + ROPE_DIM) ** -0.5
Q_SCALE = ATTN_SCALE * float(np.log2(np.e))
SSD_HEADS = 16
SSD_HEAD_DIM = 64
SSD_GROUPS = 2
SSD_STATE = 128
CONV_W = 4
SSD_WIDTH = SSD_HEADS * SSD_HEAD_DIM
GROUP_WIDTH = SSD_WIDTH // SSD_GROUPS
HEADS_PER_GROUP = SSD_HEADS // SSD_GROUPS
CONV_DIM = SSD_WIDTH + 2 * SSD_GROUPS * SSD_STATE
MLA_WIDTH = MLA_HEADS * V_DIM
QCAT = KV_LORA + 2 * ROPE_DIM
QK_HEAD = NOPE_DIM + 2 * ROPE_DIM

CUM_PIECES = 3
WIDE_PIECES = 2

LANES = 128
SUBLANES = 8
VMEM_LIMIT = 56 * 1024 * 1024


def _dot(a, b):
    return jnp.dot(a, b, preferred_element_type=F32)


def _rms(x, w):
    return x * lax.rsqrt(jnp.mean(x * x, axis=-1, keepdims=True) + EPS) * w


def _silu(x):
    return x * (1.0 / (1.0 + jnp.exp(-x)))


def _const_spec(shape):
    nd = len(shape)
    return pl.BlockSpec(shape, lambda *_: (0,) * nd)


def _pre_kernel(*refs, seq_rows, tiles_per_seq, n_alias, absorbed):
    refs = refs[n_alias:]
    (x_ref, cst_ref, nw_ref, wq_ref, wkv_ref, wg_ref, wz_ref, wxbc_ref, wsm_ref, qnw_ref, wuqn_ref, wuqp_ref,
     wuqps_ref, kvnw_ref, cw_ref, cb_ref, dtb_ref) = refs[:17]
    n_w, n_out = (1, 2) if absorbed else (2, 3)
    attn_w = refs[17:17 + n_w]
    cos_ref, sin_ref = refs[17 + n_w:19 + n_w]
    attn_out = refs[19 + n_w:19 + n_w + n_out]
    ckv_ref, kpe_ref, g_ref, z_ref, xbc_ref, dt_ref, cnew_ref, buf_sc = refs[19 + n_w + n_out:]
    keep = CONV_W - 1
    base = SUBLANES - keep
    if tiles_per_seq > 1:
        @pl.when(pl.program_id(0) % tiles_per_seq == 0)
        def _():
            buf_sc[base:SUBLANES, :] = cst_ref[0]

    x = x_ref[...]
    h = _rms(x, nw_ref[...]).astype(BF16)
    xbc = _dot(h, wxbc_ref[...])
    g_ref[...] = _silu(_dot(h, wg_ref[...]))
    z_ref[...] = _silu(_dot(h, wz_ref[...]))

    cos2 = cos_ref[...]
    sin2 = sin_ref[...]
    sm = _dot(h, wsm_ref[...])
    kpe2 = sm[:, 0:LANES] * cos2 + sm[:, LANES:2 * LANES] * sin2
    kpe_ref[...] = kpe2[:, :ROPE_DIM]
    ckv = _rms(_dot(h, wkv_ref[...]), kvnw_ref[...])
    ckv_ref[...] = ckv

    qdt = _dot(h, wq_ref[...])
    dt_in = qdt[:, Q_LORA:Q_LORA + LANES] + dtb_ref[...]
    dt_ref[...] = jnp.maximum(dt_in, 0.0) + jnp.log1p(jnp.exp(-jnp.abs(dt_in)))
    qn = _rms(qdt[:, :Q_LORA], qnw_ref[...]).astype(BF16)
    qnope = _dot(qn, wuqn_ref[...])
    pairs = MLA_HEADS // 2
    cos_all = jnp.concatenate([cos2] * pairs, axis=-1)
    sin_all = jnp.concatenate([sin2] * pairs, axis=-1)
    qpe = _dot(qn, wuqp_ref[...]) * cos_all + _dot(qn, wuqps_ref[...]) * sin_all
    lane = lax.broadcasted_iota(jnp.int32, (x.shape[0], LANES), 1)
    ckv16 = ckv.astype(BF16)
    if absorbed:
        (wuk_ref,) = attn_w
        q_ref, kcat_ref = attn_out
        kcat_ref[...] = jnp.concatenate([ckv16, kpe2.astype(BF16)], axis=-1)
    else:
        wukf_ref, wuvf_ref = attn_w
        q_ref, kh_ref, vh_ref = attn_out
        k_nope = _dot(ckv16, wukf_ref[...])
        v_all = _dot(ckv16, wuvf_ref[...])
    for hd in range(MLA_HEADS):
        cols = slice(hd * NOPE_DIM, (hd + 1) * NOPE_DIM)
        pair = qpe[:, (hd // 2) * LANES:(hd // 2 + 1) * LANES] * Q_SCALE
        own_half = (lane < ROPE_DIM) if hd % 2 == 0 else (lane >= ROPE_DIM)
        q_pe = jnp.where(own_half, pair, 0.0)
        if absorbed:
            q_main = _dot(qnope[:, cols].astype(BF16), wuk_ref[hd]) * Q_SCALE
        else:
            q_main = qnope[:, cols] * Q_SCALE
            kh_ref[hd] = jnp.concatenate([k_nope[:, cols], kpe2], axis=-1).astype(BF16)
            vh_ref[hd] = v_all[:, hd * V_DIM:(hd + 1) * V_DIM].astype(BF16)
        q_ref[hd] = jnp.concatenate([q_main, q_pe], axis=-1).astype(BF16)

    for s in range(x.shape[0] // seq_rows):
        if tiles_per_seq == 1:
            buf_sc[base:SUBLANES, :] = cst_ref[s]
        buf_sc[SUBLANES:SUBLANES + seq_rows, :] = xbc[s * seq_rows:(s + 1) * seq_rows, :]
        conv = cb_ref[...] + cw_ref[0:1, :] * buf_sc[base:base + seq_rows, :]
        for k in range(1, CONV_W):
            conv = conv + cw_ref[k:k + 1, :] * buf_sc[base + k:base + k + seq_rows, :]
        xbc_ref[s * seq_rows:(s + 1) * seq_rows, :] = _silu(conv)
        tail = buf_sc[base + seq_rows:SUBLANES + seq_rows, :]
        buf_sc[base:SUBLANES, :] = tail
        cnew_ref[s] = tail


def _pre_call(x, conv_state, layer, state_layer, stacked, w, cos2, sin2, batch, length, tr, absorbed):
    rows, d = x.shape
    depth = w["depth"]
    seq_rows = min(length, tr)
    tiles_per_seq = length // seq_rows
    nseq = tr // seq_rows
    assert rows == batch * length and length % seq_rows == 0 and tr % seq_rows == 0 and seq_rows >= CONV_W - 1
    row = lambda n: pl.BlockSpec((tr, n), lambda i: (i, 0))
    layer_row = lambda n: pl.BlockSpec((None, tr, n), lambda i: (layer, i, 0))
    heads = lambda n: pl.BlockSpec((MLA_HEADS, tr, n), lambda i: (0, i, 0))
    head_shape = lambda n: jax.ShapeDtypeStruct((MLA_HEADS, rows, n), BF16)
    if absorbed:
        attn_w = (w["w_uk_t"],)
        attn_shape = (head_shape(QCAT), jax.ShapeDtypeStruct((rows, QCAT), BF16))
        attn_specs = (heads(QCAT), row(QCAT))
    else:
        attn_w = (w["w_uk_flat"], w["w_uv_flat"])
        attn_shape = (head_shape(QK_HEAD), head_shape(QK_HEAD), head_shape(V_DIM))
        attn_specs = (heads(QK_HEAD), heads(QK_HEAD), heads(V_DIM))
    weights = (w["norm_w"], w["w_q"], w["w_kv"], w["w_g"], w["w_z"], w["w_xbc"], w["w_sm"],
               w["q_norm_w"], w["w_uq_nope"], w["w_uq_pe"], w["w_uq_pe_sw"], w["kv_norm_w"],
               w["conv_w"], w["conv_b"], w["dt_bias"]) + attn_w
    aliased = () if stacked is None else tuple(stacked)
    in_specs = ([pl.BlockSpec(memory_space=pl.ANY)] * len(aliased)
                + [row(d), pl.BlockSpec((None, nseq, CONV_W - 1, CONV_DIM),
                                        lambda i: (state_layer, i // tiles_per_seq, 0, 0))]
                + [_const_spec(a.shape) for a in weights] + [row(LANES), row(LANES)])
    out_shape = attn_shape + (
        jax.ShapeDtypeStruct((depth, rows, KV_LORA), F32),
        jax.ShapeDtypeStruct((depth, rows, ROPE_DIM), F32),
        jax.ShapeDtypeStruct((rows, MLA_WIDTH), F32),
        jax.ShapeDtypeStruct((rows, SSD_WIDTH), F32),
        jax.ShapeDtypeStruct((rows, CONV_DIM), F32),
        jax.ShapeDtypeStruct((rows, LANES), F32),
        jax.ShapeDtypeStruct((batch, CONV_W - 1, CONV_DIM), F32),
    )
    out_specs = attn_specs + (
        layer_row(KV_LORA), layer_row(ROPE_DIM), row(MLA_WIDTH), row(SSD_WIDTH), row(CONV_DIM),
        row(LANES), pl.BlockSpec((nseq, CONV_W - 1, CONV_DIM), lambda i: (i // tiles_per_seq, 0, 0)),
    )
    n_attn = len(attn_shape)
    return pl.pallas_call(
        functools.partial(_pre_kernel, seq_rows=seq_rows, tiles_per_seq=tiles_per_seq, n_alias=len(aliased),
                          absorbed=absorbed),
        grid=(rows // tr,), in_specs=in_specs, out_specs=out_specs, out_shape=out_shape,
        scratch_shapes=[pltpu.VMEM((seq_rows + SUBLANES, CONV_DIM), F32)],
        input_output_aliases={k: n_attn + k for k in range(len(aliased))},
        compiler_params=pltpu.CompilerParams(dimension_semantics=("arbitrary",), vmem_limit_bytes=VMEM_LIMIT),
        name="pre",
    )(*aliased, x, conv_state, *weights, cos2, sin2)


def _scores(q, kblk):
    return lax.dot_general(q, kblk, (((1,), (1,)), ((), ())), preferred_element_type=F32)


def _softmax_pv(s, v, m_sc, l_sc, acc_sc, allowed):
    rows = slice(0, s.shape[0])
    if allowed is not None:
        s = jnp.where(allowed, s, -jnp.inf)
    tk = s.shape[1]
    m_prev = m_sc[rows, :]
    if tk % LANES == 0:
        chunks = [s[:, c * LANES:(c + 1) * LANES] for c in range(tk // LANES)]
        m_new = jnp.maximum(m_prev, jnp.max(functools.reduce(jnp.maximum, chunks), axis=-1, keepdims=True))
        ps = [jnp.exp2(c - m_new) for c in chunks]
        psum = functools.reduce(jnp.add, ps)
        p = jnp.concatenate(ps, axis=-1)
    else:
        m_new = jnp.maximum(m_prev, jnp.max(s, axis=-1, keepdims=True))
        p = jnp.exp2(s - m_new[:, :1])
        psum = jnp.sum(p, axis=-1, keepdims=True) * (1.0 / LANES)
    alpha = jnp.exp2(m_prev - m_new)
    l_sc[rows, :] = alpha * l_sc[rows, :] + psum
    pv = _dot(p.astype(BF16), v)
    acc_sc[rows, :] = jnp.concatenate([alpha] * (acc_sc.shape[1] // LANES), axis=-1) * acc_sc[rows, :] + pv
    m_sc[rows, :] = m_new


def _attn_finish(l_sc, acc_sc, g_ref, wuv_ref, o_ref, tq):
    for hd in range(MLA_HEADS):
        rows = slice(hd * tq, (hd + 1) * tq)
        cols = slice(hd * V_DIM, (hd + 1) * V_DIM)
        inv = 1.0 / jnp.sum(l_sc[rows, :], axis=-1, keepdims=True)
        o = _dot((acc_sc[rows, :] * inv).astype(BF16), wuv_ref[hd])
        o_ref[:, cols] = (o * g_ref[:, cols]).astype(o_ref.dtype)


def _attn_init(m_sc, l_sc, acc_sc):
    m_sc[...] = jnp.full(m_sc.shape, -jnp.inf, F32)
    l_sc[...] = jnp.zeros(l_sc.shape, F32)
    acc_sc[...] = jnp.zeros(acc_sc.shape, F32)


def _attn_scratch(rows, width):
    return [pltpu.VMEM((rows, LANES), F32), pltpu.VMEM((rows, LANES), F32), pltpu.VMEM((rows, width), F32)]


def _attn_prompt_kernel(q_ref, k_ref, v_ref, g_ref, o_ref, m_sc, l_sc, acc_sc, *s_sc, tq, tk):
    i = pl.program_id(1)
    _attn_init(m_sc, l_sc, acc_sc)
    q = q_ref[...]
    span = 2 * tk
    own = tq // span
    npair = (i + 1) * own

    def step(cur, nxt, p, masked):
        pn = jnp.minimum(p + 1, npair - 1)
        for half, s_ref in enumerate(nxt):
            s_ref[...] = _scores(q, k_ref[pl.ds(pl.multiple_of(pn * span + half * tk, tk), tk), :])
        allowed = None
        if masked:
            qchunk = lax.broadcasted_iota(jnp.int32, (tq, 1), 0) // CHUNK
            kchunk = (p * span - i * tq + lax.broadcasted_iota(jnp.int32, (1, span), 1)) // CHUNK
            allowed = kchunk <= qchunk
        s = jnp.concatenate([s_ref[...] for s_ref in cur], axis=-1)
        _softmax_pv(s, v_ref[pl.ds(pl.multiple_of(p * span, span), span), :], m_sc, l_sc, acc_sc, allowed)

    sa, sb = s_sc[:2], s_sc[2:]
    for half, s_ref in enumerate(sa):
        s_ref[...] = _scores(q, k_ref[half * tk:(half + 1) * tk, :])

    def visible(t, carry):
        step(sa, sb, 2 * t, False)
        step(sb, sa, 2 * t + 1, False)
        return carry

    def masked(t, carry):
        step(sa, sb, 2 * t, True)
        step(sb, sa, 2 * t + 1, True)
        return carry

    lax.fori_loop(0, i * own // 2, visible, 0)
    lax.fori_loop(i * own // 2, npair // 2, masked, 0)
    inv = 1.0 / jnp.sum(l_sc[...], axis=-1, keepdims=True)
    o_ref[...] = (acc_sc[...] * inv * g_ref[...]).astype(o_ref.dtype)


def _attn_prompt_call(qh, kh, vh, g, tq, tk):
    heads, rows, _ = qh.shape
    assert tk % CHUNK == 0 and rows % tq == 0 and tq % (4 * tk) == 0
    per_head = lambda n, m: pl.BlockSpec((None, n, m), lambda hd, i: (hd, 0, 0), pipeline_mode=pl.Buffered(1))
    return pl.pallas_call(
        functools.partial(_attn_prompt_kernel, tq=tq, tk=tk),
        grid=(heads, rows // tq),
        in_specs=[pl.BlockSpec((None, tq, QK_HEAD), lambda hd, i: (hd, i, 0)),
                  per_head(rows, QK_HEAD), per_head(rows, V_DIM),
                  pl.BlockSpec((tq, V_DIM), lambda hd, i: (i, hd))],
        out_specs=pl.BlockSpec((tq, V_DIM), lambda hd, i: (i, hd)),
        out_shape=jax.ShapeDtypeStruct((rows, heads * V_DIM), BF16),
        scratch_shapes=_attn_scratch(tq, V_DIM) + [pltpu.VMEM((tq, tk), F32)] * 4,
        compiler_params=pltpu.CompilerParams(dimension_semantics=("arbitrary", "arbitrary"),
                                             vmem_limit_bytes=VMEM_LIMIT),
        name="attn_prompt",
    )(qh, kh, vh, g)


def _attn_sample_kernel(q_ref, pckv_ref, pkpe_ref, knew_ref, g_ref, wuv_ref, o_ref, m_sc, l_sc, acc_sc, *, tq, tk):
    q = q_ref[...].reshape(MLA_HEADS * tq, QCAT)
    _attn_init(m_sc, l_sc, acc_sc)
    past = pckv_ref.shape[0]

    def body(j, carry):
        r0 = pl.multiple_of(j * tk, tk)
        kpe = pkpe_ref[pl.ds(r0, tk), :]
        kblk = jnp.concatenate([pckv_ref[pl.ds(r0, tk), :], kpe, kpe], axis=-1).astype(BF16)
        _softmax_pv(_scores(q, kblk), kblk[:, :KV_LORA], m_sc, l_sc, acc_sc, None)
        return carry

    lax.fori_loop(0, past // tk, body, 0)
    knew = knew_ref[...]
    _softmax_pv(_scores(q, knew), knew[:, :KV_LORA], m_sc, l_sc, acc_sc, None)
    _attn_finish(l_sc, acc_sc, g_ref, wuv_ref, o_ref, tq)


def _attn_sample_call(qcat, past_ckv, past_kpe, layer, kcat, g, wuv_t, batch, length, tk):
    past = past_ckv.shape[2]
    assert past % CHUNK == 0 and length <= CHUNK and past % tk == 0
    rows = batch * length
    r8 = MLA_HEADS * length
    return pl.pallas_call(
        functools.partial(_attn_sample_kernel, tq=length, tk=tk),
        grid=(batch,),
        in_specs=[pl.BlockSpec((MLA_HEADS, length, QCAT), lambda b: (0, b, 0)),
                  pl.BlockSpec((None, None, past, KV_LORA), lambda b: (layer, b, 0, 0)),
                  pl.BlockSpec((None, None, past, ROPE_DIM), lambda b: (layer, b, 0, 0)),
                  pl.BlockSpec((length, QCAT), lambda b: (b, 0)),
                  pl.BlockSpec((length, MLA_WIDTH), lambda b: (b, 0)),
                  _const_spec(wuv_t.shape)],
        out_specs=pl.BlockSpec((length, MLA_WIDTH), lambda b: (b, 0)),
        out_shape=jax.ShapeDtypeStruct((rows, MLA_WIDTH), BF16),
        scratch_shapes=_attn_scratch(r8, KV_LORA),
        compiler_params=pltpu.CompilerParams(dimension_semantics=("arbitrary",), vmem_limit_bytes=VMEM_LIMIT),
        name="attn_sample",
    )(qcat, past_ckv, past_kpe, kcat, g, wuv_t)


def _ssd_constants(cl):
    hb = LANES // cl
    ltri = np.tril(np.ones((cl, cl), np.float32))
    exp_s = np.zeros((LANES, SSD_HEADS * cl), np.float32)
    exp_p = np.zeros((LANES, SSD_WIDTH), np.float32)
    for hd in range(SSD_HEADS):
        exp_s[hd, hd * cl:(hd + 1) * cl] = 1.0
        exp_p[hd, hd * SSD_HEAD_DIM:(hd + 1) * SSD_HEAD_DIM] = 1.0
    t = np.arange(cl)[:, None]
    s = np.tile(np.arange(cl), SSD_HEADS)[None, :]
    eye = (t == s).astype(np.float32)
    tril = (t >= s).astype(np.float32)
    hrow = np.repeat(np.arange(hb), cl)[:, None]
    hcol = np.repeat(np.arange(hb), SSD_HEAD_DIM)[None, :]
    bdiag = (hrow == hcol).astype(np.float32)
    bf = lambda a: jnp.asarray(a, BF16)
    return (bf(np.tile(ltri, (1, CUM_PIECES))), bf(np.tile(exp_s, (CUM_PIECES, 1))),
            bf(np.tile(exp_p, (WIDE_PIECES, 1))), jnp.asarray(eye), jnp.asarray(tril), bf(bdiag))


def _split(x, n):
    pieces = []
    for _ in range(n):
        p = x.astype(BF16)
        pieces.append(p)
        x = x - p.astype(F32)
    return pieces


def _expand(x, sel, n):
    return _dot(jnp.concatenate(_split(x, n), axis=-1), sel)


def _ssd_kernel(act_ref, dt_ref, z_ref, h0_ref, alog_ref, dsk_ref, nw_ref,
                ltri_ref, exps_ref, expp_ref, eye_ref, tril_ref, bdiag_ref,
                y_ref, hout_ref, h_sc, wide_sc, e_sc, *, cl, cps):
    step = pl.program_id(1)
    last = pl.num_programs(1) - 1
    rows = cl * cps

    @pl.when(step == 0)
    def _():
        h_sc[...] = h0_ref[...]

    dt = dt_ref[...]
    a_pieces = _split(dt * -jnp.exp(alog_ref[...]), CUM_PIECES)
    cums, totals = [], []
    for c in range(cps):
        stacked = jnp.concatenate([p[c * cl:(c + 1) * cl, :] for p in a_pieces], axis=0)
        cum_c = _dot(ltri_ref[...], stacked)
        cums.append(cum_c)
        totals.append(jnp.broadcast_to(cum_c[cl - 1:cl, :], cum_c.shape))
    cum = jnp.concatenate(cums, axis=0)
    total = jnp.concatenate(totals, axis=0)
    wide_sc[...] = _expand(jnp.concatenate([dt, jnp.exp(total - cum), jnp.exp(cum)], axis=0), expp_ref[...],
                           WIDE_PIECES)
    e_sc[...] = _expand(cum, exps_ref[...], CUM_PIECES)

    hb = LANES // cl
    gs = GROUP_WIDTH
    gm = HEADS_PER_GROUP * cl

    def chunk(c, carry):
        r0 = pl.multiple_of(c * cl, cl)
        xs = act_ref[pl.ds(r0, cl), 0:SSD_WIDTH]
        bm = act_ref[pl.ds(r0, cl), SSD_WIDTH:SSD_WIDTH + SSD_GROUPS * SSD_STATE]
        cm = act_ref[pl.ds(r0, cl), SSD_WIDTH + SSD_GROUPS * SSD_STATE:CONV_DIM]
        dt_w = wide_sc[pl.ds(r0, cl), :]
        dstate_w = wide_sc[pl.ds(pl.multiple_of(rows + r0, cl), cl), :]
        ecum_w = wide_sc[pl.ds(pl.multiple_of(2 * rows + r0, cl), cl), :]
        e = e_sc[pl.ds(r0, cl), :]
        f_row = jnp.sum(jnp.where(eye_ref[...] > 0.0, e, 0.0), axis=0, keepdims=True)
        decay = jnp.where(tril_ref[...] > 0.0, jnp.exp(e - f_row), 0.0)
        xd = xs * dt_w
        w_state = (xd * dstate_w).astype(BF16)
        xd16 = xd.astype(BF16)
        h_prev = h_sc[...]
        h16 = h_prev.astype(BF16)
        bdiag = bdiag_ref[...]
        y_parts, st_parts = [], []
        for g in range(SSD_GROUPS):
            bg = bm[:, g * SSD_STATE:(g + 1) * SSD_STATE].astype(BF16)
            cg = cm[:, g * SSD_STATE:(g + 1) * SSD_STATE].astype(BF16)
            cb = lax.dot_general(cg, jnp.concatenate([bg] * hb, axis=0), (((1,), (1,)), ((), ())),
                                 preferred_element_type=F32)
            y_off = _dot(cg, h16[:, g * gs:(g + 1) * gs]) * ecum_w[:, g * gs:(g + 1) * gs]
            diag = []
            for j in range(gm // LANES):
                m = (cb * decay[:, g * gm + j * LANES:g * gm + (j + 1) * LANES]).astype(BF16)
                c0 = g * gs + j * hb * SSD_HEAD_DIM
                xj = xd16[:, c0:c0 + hb * SSD_HEAD_DIM]
                diag.append(_dot(m, jnp.concatenate([xj] * hb, axis=0) * bdiag))
            y_parts.append(y_off + jnp.concatenate(diag, axis=-1))
            st_parts.append(lax.dot_general(bg, w_state[:, g * gs:(g + 1) * gs], (((0,), (0,)), ((), ())),
                                            preferred_element_type=F32))
        h_sc[...] = h_prev * ecum_w[cl - 1:cl, :] + jnp.concatenate(st_parts, axis=-1)
        y = jnp.concatenate(y_parts, axis=-1) + xs * dsk_ref[...]
        y = y * z_ref[pl.ds(r0, cl), :]
        normed = [_rms(y[:, g * gs:(g + 1) * gs], nw_ref[:, g * gs:(g + 1) * gs]) for g in range(SSD_GROUPS)]
        y_ref[pl.ds(r0, cl), :] = jnp.concatenate(normed, axis=-1).astype(y_ref.dtype)
        return carry

    lax.fori_loop(0, cps, chunk, 0, unroll=True)

    @pl.when(step == last)
    def _():
        hout_ref[...] = h_sc[...]


def _ssd_call(act, dt, z, h0_t, state_layer, w, batch, length, cps):
    cl = min(CHUNK, length)
    rows = cl * cps
    assert length % rows == 0 and LANES % cl == 0
    consts = _ssd_constants(cl)
    seq = lambda n: pl.BlockSpec((None, rows, n), lambda b, s: (b, s, 0))
    params = (w["a_log"], w["d_skip"], w["ssd_norm_w"])
    in_specs = ([seq(CONV_DIM), seq(LANES), seq(SSD_WIDTH),
                 pl.BlockSpec((None, None, SSD_STATE, SSD_WIDTH), lambda b, s: (state_layer, b, 0, 0))]
                + [pl.BlockSpec(a.shape, lambda b, s: (0, 0)) for a in params + consts])
    out_shape = (jax.ShapeDtypeStruct((batch, length, SSD_WIDTH), BF16),
                 jax.ShapeDtypeStruct((batch, SSD_STATE, SSD_WIDTH), F32))
    out_specs = (seq(SSD_WIDTH), pl.BlockSpec((None, SSD_STATE, SSD_WIDTH), lambda b, s: (b, 0, 0)))
    return pl.pallas_call(
        functools.partial(_ssd_kernel, cl=cl, cps=cps),
        grid=(batch, length // rows), in_specs=in_specs, out_specs=out_specs, out_shape=out_shape,
        scratch_shapes=[pltpu.VMEM((SSD_STATE, SSD_WIDTH), F32), pltpu.VMEM((3 * rows, SSD_WIDTH), F32),
                        pltpu.VMEM((rows, SSD_HEADS * cl), F32)],
        compiler_params=pltpu.CompilerParams(dimension_semantics=("arbitrary", "arbitrary"),
                                             vmem_limit_bytes=VMEM_LIMIT),
        name="ssd",
    )(act.reshape(batch, length, CONV_DIM), dt.reshape(batch, length, LANES),
      z.reshape(batch, length, SSD_WIDTH), h0_t, *params, *consts)


def _post_kernel(mla_ref, ssd_ref, x_ref, wa_ref, wb_ref, fw_ref, o_ref, *, final):
    y = x_ref[...] + _dot(mla_ref[...], wa_ref[...]) + _dot(ssd_ref[...], wb_ref[...])
    o_ref[...] = _rms(y, fw_ref[...]) if final else y


def _post_call(mla, ssd, x, w, final_norm_w, final, tr):
    rows, d = x.shape
    row = lambda n: pl.BlockSpec((tr, n), lambda i: (i, 0))
    return pl.pallas_call(
        functools.partial(_post_kernel, final=final),
        grid=(rows // tr,),
        in_specs=[row(MLA_WIDTH), row(SSD_WIDTH), row(d), _const_spec(w["w_out_mla"].shape),
                  _const_spec(w["w_out_ssd"].shape), _const_spec(final_norm_w.shape)],
        out_specs=row(d), out_shape=jax.ShapeDtypeStruct((rows, d), F32),
        compiler_params=pltpu.CompilerParams(dimension_semantics=("arbitrary",), vmem_limit_bytes=VMEM_LIMIT),
        name="post",
    )(mla, ssd, x, w["w_out_mla"], w["w_out_ssd"], final_norm_w)


def _swap_halves(wcols):
    k, n = wcols.shape
    blocks = wcols.reshape(k, n // ROPE_DIM, 2, ROPE_DIM // 2)
    return blocks[:, :, ::-1, :].reshape(k, n)


def _layer_weights(i, norm_w, w_in, q_norm_w, w_uq, kv_norm_w, w_uk, w_uv, conv_w, conv_b, dt_bias, a_log, d_skip,
                   ssd_norm_w, w_out):
    d = w_in.shape[1]
    offs = np.cumsum((0, Q_LORA, KV_LORA, ROPE_DIM, MLA_WIDTH, SSD_WIDTH, CONV_DIM, SSD_HEADS))
    col = lambda j: w_in[i][:, offs[j]:offs[j + 1]]
    w_kpe = col(2)
    w_kpe_sw = _swap_halves(w_kpe)
    w_sm = jnp.concatenate([w_kpe, w_kpe, w_kpe_sw, w_kpe_sw], axis=1)
    w_qdt = jnp.concatenate([col(0), col(6), jnp.zeros((d, LANES - SSD_HEADS), F32)], axis=1)
    uq = w_uq[i].reshape(Q_LORA, MLA_HEADS, NOPE_DIM + ROPE_DIM)
    uq_pe = uq[:, :, NOPE_DIM:].reshape(Q_LORA, MLA_HEADS * ROPE_DIM)
    pad_heads = lambda v: jnp.concatenate([v, jnp.zeros((LANES - SSD_HEADS,), F32)])[None, :]
    return {
        "depth": w_in.shape[0],
        "norm_w": norm_w[i][None, :],
        "w_q": w_qdt.astype(BF16), "w_kv": col(1).astype(BF16), "w_g": col(3).astype(BF16),
        "w_z": col(4).astype(BF16), "w_xbc": col(5).astype(BF16), "w_sm": w_sm.astype(BF16),
        "q_norm_w": q_norm_w[i][None, :],
        "w_uq_nope": uq[:, :, :NOPE_DIM].reshape(Q_LORA, MLA_HEADS * NOPE_DIM).astype(BF16),
        "w_uq_pe": uq_pe.astype(BF16), "w_uq_pe_sw": _swap_halves(uq_pe).astype(BF16),
        "kv_norm_w": kv_norm_w[i][None, :],
        "w_uk_t": jnp.transpose(w_uk[i], (1, 2, 0)).astype(BF16),
        "w_uv_t": jnp.transpose(w_uv[i], (1, 0, 2)).astype(BF16),
        "w_uk_flat": w_uk[i].reshape(KV_LORA, MLA_HEADS * NOPE_DIM).astype(BF16),
        "w_uv_flat": w_uv[i].reshape(KV_LORA, MLA_HEADS * V_DIM).astype(BF16),
        "conv_w": conv_w[i], "conv_b": conv_b[i][None, :],
        "dt_bias": pad_heads(dt_bias[i]), "a_log": pad_heads(a_log[i]),
        "d_skip": jnp.repeat(d_skip[i], SSD_HEAD_DIM)[None, :],
        "ssd_norm_w": ssd_norm_w[i][None, :],
        "w_out_mla": w_out[i][:MLA_WIDTH].astype(BF16), "w_out_ssd": w_out[i][MLA_WIDTH:].astype(BF16),
    }


def _rope_tables(past, length, batch):
    half = ROPE_DIM // 2
    inv = 1.0 / (ROPE_THETA ** (np.arange(half, dtype=np.float64) / half))
    ang = (past + np.arange(length, dtype=np.float64))[:, None] * inv[None, :]
    cos, sin = np.cos(ang), np.sin(ang)
    reps = LANES // ROPE_DIM
    cos2 = np.tile(np.concatenate([cos, cos], axis=-1), (batch, reps))
    sin2 = np.tile(np.concatenate([-sin, sin], axis=-1), (batch, reps))
    return jnp.asarray(cos2, F32), jnp.asarray(sin2, F32)


def _state_to_lanes(h):
    b = h.shape[0]
    return jnp.transpose(h, (0, 3, 1, 2)).reshape(b, SSD_STATE, SSD_WIDTH)


def _state_from_lanes(ht):
    b = ht.shape[0]
    return jnp.transpose(ht.reshape(b, SSD_STATE, SSD_HEADS, SSD_HEAD_DIM), (0, 2, 3, 1))


def _pick(n, prefs):
    for p in prefs:
        if n % p == 0:
            return p
    return n


def _layer(x, batch, length, layer, state_layer, caches, conv_state, ssm_state_t, stacked, w, tables, final_norm_w):
    rows = batch * length
    tr = _pick(rows, (256, 128))
    absorbed = caches is not None
    *attn_in, ckv, kpe, g, z, act, dt, conv_new = _pre_call(x, conv_state, layer, state_layer, stacked, w, tables[0],
                                                            tables[1], batch, length, tr, absorbed)
    if absorbed:
        qcat, kcat = attn_in
        mla = _attn_sample_call(qcat, caches[0], caches[1], layer, kcat, g, w["w_uv_t"], batch, length,
                                _pick(caches[0].shape[2], (2048, 1024, 512, 256, 128, 64)))
    else:
        assert batch == 1
        tq = _pick(length, (2048, 1024, 512))
        mla = _attn_prompt_call(*attn_in, g, tq, min(tq // 4, 512))
    cl = min(CHUNK, length)
    ssd, h_t = _ssd_call(act, dt, z, ssm_state_t, state_layer, w, batch, length, _pick(length // cl, (4, 2, 1)))
    y = _post_call(mla, ssd.reshape(rows, SSD_WIDTH), x, w, final_norm_w, layer == w["depth"] - 1, tr)
    return y, (ckv, kpe), conv_new, h_t


def kernel(x_prompt, x_sample, cache_ckv, cache_kpe, state_conv, state_ssm, norm_w, w_in, q_norm_w, w_uq,
           kv_norm_w, w_uk, w_uv, conv_w, conv_b, dt_bias, a_log, d_skip, ssd_norm_w, w_out, final_norm_w):
    depth = w_in.shape[0]
    bp, lp, d = x_prompt.shape
    bs, ls, _ = x_sample.shape
    past = cache_ckv.shape[2]
    tab_p = _rope_tables(0, lp, bp)
    tab_s = _rope_tables(past, ls, bs)
    fw = final_norm_w[None, :]
    yp, ys = x_prompt.reshape(bp * lp, d), x_sample.reshape(bs * ls, d)
    zero_conv = jnp.zeros((1, bp, CONV_W - 1, CONV_DIM), F32)
    zero_ssm = jnp.zeros((1, bp, SSD_STATE, SSD_WIDTH), F32)
    ssm_t = _state_to_lanes(state_ssm.reshape((depth * bs,) + state_ssm.shape[2:]))
    ssm_t = ssm_t.reshape(depth, bs, SSD_STATE, SSD_WIDTH)
    kv_p = kv_s = None
    conv_p, conv_s, ssm_p, ssm_s = [], [], [], []
    for i in range(depth):
        w = _layer_weights(i, norm_w, w_in, q_norm_w, w_uq, kv_norm_w, w_uk, w_uv, conv_w, conv_b, dt_bias, a_log,
                           d_skip, ssd_norm_w, w_out)
        yp, kv_p, c_new, h_new = _layer(yp, bp, lp, i, 0, None, zero_conv, zero_ssm, kv_p, w, tab_p, fw)
        conv_p.append(c_new)
        ssm_p.append(_state_from_lanes(h_new))
        ys, kv_s, c_new, h_new = _layer(ys, bs, ls, i, i, (cache_ckv, cache_kpe), state_conv, ssm_t, kv_s, w,
                                        tab_s, fw)
        conv_s.append(c_new)
        ssm_s.append(_state_from_lanes(h_new))
    return (yp.reshape(bp, lp, d), ys.reshape(bs, ls, d),
            kv_p[0].reshape(depth, bp, lp, KV_LORA), kv_p[1].reshape(depth, bp, lp, ROPE_DIM),
            jnp.stack(conv_p), jnp.stack(ssm_p),
            kv_s[0].reshape(depth, bs, ls, KV_LORA), kv_s[1].reshape(depth, bs, ls, ROPE_DIM),
            jnp.stack(conv_s), jnp.stack(ssm_s))
```

```python
import functools

import numpy as np
import jax
import jax.numpy as jnp
from jax import lax
from jax.experimental import pallas as pl
from jax.experimental.pallas import tpu as pltpu

F32 = jnp.float32
BF16 = jnp.bfloat16

EPS = 1e-6
CHUNK = 64
MLA_HEADS = 8
Q_LORA = 384
KV_LORA = 256
NOPE_DIM = 128
ROPE_DIM = 64
V_DIM = 128
ROPE_THETA = 10000.0
ATTN_SCALE = (NOPE_DIM + ROPE_DIM) ** -0.5
Q_SCALE = ATTN_SCALE * float(np.log2(np.e))
SSD_HEADS = 16
SSD_HEAD_DIM = 64
SSD_GROUPS = 2
SSD_STATE = 128
CONV_W = 4
SSD_WIDTH = SSD_HEADS * SSD_HEAD_DIM
GROUP_WIDTH = SSD_WIDTH // SSD_GROUPS
HEADS_PER_GROUP = SSD_HEADS // SSD_GROUPS
CONV_DIM = SSD_WIDTH + 2 * SSD_GROUPS * SSD_STATE
MLA_WIDTH = MLA_HEADS * V_DIM
QCAT = KV_LORA + 2 * ROPE_DIM
QK_HEAD = NOPE_DIM + 2 * ROPE_DIM

CUM_PIECES = 3
WIDE_PIECES = 2

LANES = 128
SUBLANES = 8
VMEM_LIMIT = 56 * 1024 * 1024


def _dot(a, b):
    return jnp.dot(a, b, preferred_element_type=F32)


def _rms(x, w):
    return x * lax.rsqrt(jnp.mean(x * x, axis=-1, keepdims=True) + EPS) * w


def _silu(x):
    return x * (1.0 / (1.0 + jnp.exp(-x)))


def _const_spec(shape):
    nd = len(shape)
    return pl.BlockSpec(shape, lambda *_: (0,) * nd)


def _pre_kernel(*refs, seq_rows, tiles_per_seq, n_alias, absorbed):
    refs = refs[n_alias:]
    (x_ref, cst_ref, nw_ref, wq_ref, wkv_ref, wg_ref, wz_ref, wxbc_ref, wsm_ref, qnw_ref, wuqn_ref, wuqp_ref,
     wuqps_ref, kvnw_ref, cw_ref, cb_ref, dtb_ref) = refs[:17]
    n_w, n_out = (1, 2) if absorbed else (2, 3)
    attn_w = refs[17:17 + n_w]
    cos_ref, sin_ref = refs[17 + n_w:19 + n_w]
    attn_out = refs[19 + n_w:19 + n_w + n_out]
    ckv_ref, kpe_ref, g_ref, z_ref, xbc_ref, dt_ref, cnew_ref, buf_sc = refs[19 + n_w + n_out:]
    keep = CONV_W - 1
    base = SUBLANES - keep
    if tiles_per_seq > 1:
        @pl.when(pl.program_id(0) % tiles_per_seq == 0)
        def _():
            buf_sc[base:SUBLANES, :] = cst_ref[0]

    x = x_ref[...]
    h = _rms(x, nw_ref[...]).astype(BF16)
    xbc = _dot(h, wxbc_ref[...])
    g_ref[...] = _silu(_dot(h, wg_ref[...]))
    z_ref[...] = _silu(_dot(h, wz_ref[...]))

    cos2 = cos_ref[...]
    sin2 = sin_ref[...]
    sm = _dot(h, wsm_ref[...])
    kpe2 = sm[:, 0:LANES] * cos2 + sm[:, LANES:2 * LANES] * sin2
    kpe_ref[...] = kpe2[:, :ROPE_DIM]
    ckv = _rms(_dot(h, wkv_ref[...]), kvnw_ref[...])
    ckv_ref[...] = ckv

    qdt = _dot(h, wq_ref[...])
    dt_in = qdt[:, Q_LORA:Q_LORA + LANES] + dtb_ref[...]
    dt_ref[...] = jnp.maximum(dt_in, 0.0) + jnp.log1p(jnp.exp(-jnp.abs(dt_in)))
    qn = _rms(qdt[:, :Q_LORA], qnw_ref[...]).astype(BF16)
    qnope = _dot(qn, wuqn_ref[...])
    pairs = MLA_HEADS // 2
    cos_all = jnp.concatenate([cos2] * pairs, axis=-1)
    sin_all = jnp.concatenate([sin2] * pairs, axis=-1)
    qpe = _dot(qn, wuqp_ref[...]) * cos_all + _dot(qn, wuqps_ref[...]) * sin_all
    lane = lax.broadcasted_iota(jnp.int32, (x.shape[0], LANES), 1)
    ckv16 = ckv.astype(BF16)
    if absorbed:
        (wuk_ref,) = attn_w
        q_ref, kcat_ref = attn_out
        kcat_ref[...] = jnp.concatenate([ckv16, kpe2.astype(BF16)], axis=-1)
    else:
        wukf_ref, wuvf_ref = attn_w
        q_ref, kh_ref, vh_ref = attn_out
        k_nope = _dot(ckv16, wukf_ref[...])
        v_all = _dot(ckv16, wuvf_ref[...])
    for hd in range(MLA_HEADS):
        cols = slice(hd * NOPE_DIM, (hd + 1) * NOPE_DIM)
        pair = qpe[:, (hd // 2) * LANES:(hd // 2 + 1) * LANES] * Q_SCALE
        own_half = (lane < ROPE_DIM) if hd % 2 == 0 else (lane >= ROPE_DIM)
        q_pe = jnp.where(own_half, pair, 0.0)
        if absorbed:
            q_main = _dot(qnope[:, cols].astype(BF16), wuk_ref[hd]) * Q_SCALE
        else:
            q_main = qnope[:, cols] * Q_SCALE
            kh_ref[hd] = jnp.concatenate([k_nope[:, cols], kpe2], axis=-1).astype(BF16)
            vh_ref[hd] = v_all[:, hd * V_DIM:(hd + 1) * V_DIM].astype(BF16)
        q_ref[hd] = jnp.concatenate([q_main, q_pe], axis=-1).astype(BF16)

    for s in range(x.shape[0] // seq_rows):
        if tiles_per_seq == 1:
            buf_sc[base:SUBLANES, :] = cst_ref[s]
        buf_sc[SUBLANES:SUBLANES + seq_rows, :] = xbc[s * seq_rows:(s + 1) * seq_rows, :]
        conv = cb_ref[...] + cw_ref[0:1, :] * buf_sc[base:base + seq_rows, :]
        for k in range(1, CONV_W):
            conv = conv + cw_ref[k:k + 1, :] * buf_sc[base + k:base + k + seq_rows, :]
        xbc_ref[s * seq_rows:(s + 1) * seq_rows, :] = _silu(conv)
        tail = buf_sc[base + seq_rows:SUBLANES + seq_rows, :]
        buf_sc[base:SUBLANES, :] = tail
        cnew_ref[s] = tail


def _pre_call(x, conv_state, layer, state_layer, stacked, w, cos2, sin2, batch, length, tr, absorbed):
    rows, d = x.shape
    depth = w["depth"]
    seq_rows = min(length, tr)
    tiles_per_seq = length // seq_rows
    nseq = tr // seq_rows
    assert rows == batch * length and length % seq_rows == 0 and tr % seq_rows == 0 and seq_rows >= CONV_W - 1
    row = lambda n: pl.BlockSpec((tr, n), lambda i: (i, 0))
    layer_row = lambda n: pl.BlockSpec((None, tr, n), lambda i: (layer, i, 0))
    heads = lambda n: pl.BlockSpec((MLA_HEADS, tr, n), lambda i: (0, i, 0))
    head_shape = lambda n: jax.ShapeDtypeStruct((MLA_HEADS, rows, n), BF16)
    if absorbed:
        attn_w = (w["w_uk_t"],)
        attn_shape = (head_shape(QCAT), jax.ShapeDtypeStruct((rows, QCAT), BF16))
        attn_specs = (heads(QCAT), row(QCAT))
    else:
        attn_w = (w["w_uk_flat"], w["w_uv_flat"])
        attn_shape = (head_shape(QK_HEAD), head_shape(QK_HEAD), head_shape(V_DIM))
        attn_specs = (heads(QK_HEAD), heads(QK_HEAD), heads(V_DIM))
    weights = (w["norm_w"], w["w_q"], w["w_kv"], w["w_g"], w["w_z"], w["w_xbc"], w["w_sm"],
               w["q_norm_w"], w["w_uq_nope"], w["w_uq_pe"], w["w_uq_pe_sw"], w["kv_norm_w"],
               w["conv_w"], w["conv_b"], w["dt_bias"]) + attn_w
    aliased = () if stacked is None else tuple(stacked)
    in_specs = ([pl.BlockSpec(memory_space=pl.ANY)] * len(aliased)
                + [row(d), pl.BlockSpec((None, nseq, CONV_W - 1, CONV_DIM),
                                        lambda i: (state_layer, i // tiles_per_seq, 0, 0))]
                + [_const_spec(a.shape) for a in weights] + [row(LANES), row(LANES)])
    out_shape = attn_shape + (
        jax.ShapeDtypeStruct((depth, rows, KV_LORA), F32),
        jax.ShapeDtypeStruct((depth, rows, ROPE_DIM), F32),
        jax.ShapeDtypeStruct((rows, MLA_WIDTH), F32),
        jax.ShapeDtypeStruct((rows, SSD_WIDTH), F32),
        jax.ShapeDtypeStruct((rows, CONV_DIM), F32),
        jax.ShapeDtypeStruct((rows, LANES), F32),
        jax.ShapeDtypeStruct((batch, CONV_W - 1, CONV_DIM), F32),
    )
    out_specs = attn_specs + (
        layer_row(KV_LORA), layer_row(ROPE_DIM), row(MLA_WIDTH), row(SSD_WIDTH), row(CONV_DIM),
        row(LANES), pl.BlockSpec((nseq, CONV_W - 1, CONV_DIM), lambda i: (i // tiles_per_seq, 0, 0)),
    )
    n_attn = len(attn_shape)
    return pl.pallas_call(
        functools.partial(_pre_kernel, seq_rows=seq_rows, tiles_per_seq=tiles_per_seq, n_alias=len(aliased),
                          absorbed=absorbed),
        grid=(rows // tr,), in_specs=in_specs, out_specs=out_specs, out_shape=out_shape,
        scratch_shapes=[pltpu.VMEM((seq_rows + SUBLANES, CONV_DIM), F32)],
        input_output_aliases={k: n_attn + k for k in range(len(aliased))},
        compiler_params=pltpu.CompilerParams(dimension_semantics=("arbitrary",), vmem_limit_bytes=VMEM_LIMIT),
        name="pre",
    )(*aliased, x, conv_state, *weights, cos2, sin2)


def _scores(q, kblk):
    return lax.dot_general(q, kblk, (((1,), (1,)), ((), ())), preferred_element_type=F32)


def _softmax_pv(s, v, m_sc, l_sc, acc_sc, allowed, row0=0):
    rows = slice(row0, row0 + s.shape[0])
    if allowed is not None:
        s = jnp.where(allowed, s, -jnp.inf)
    tk = s.shape[1]
    m_prev = m_sc[rows, :]
    if tk % LANES == 0:
        chunks = [s[:, c * LANES:(c + 1) * LANES] for c in range(tk // LANES)]
        m_new = jnp.maximum(m_prev, jnp.max(functools.reduce(jnp.maximum, chunks), axis=-1, keepdims=True))
        ps = [jnp.exp2(c - m_new) for c in chunks]
        psum = functools.reduce(jnp.add, ps)
        p = jnp.concatenate(ps, axis=-1)
    else:
        m_new = jnp.maximum(m_prev, jnp.max(s, axis=-1, keepdims=True))
        p = jnp.exp2(s - m_new[:, :1])
        psum = jnp.sum(p, axis=-1, keepdims=True) * (1.0 / LANES)
    alpha = jnp.exp2(m_prev - m_new)
    l_sc[rows, :] = alpha * l_sc[rows, :] + psum
    pv = _dot(p.astype(BF16), v)
    acc_sc[rows, :] = jnp.concatenate([alpha] * (acc_sc.shape[1] // LANES), axis=-1) * acc_sc[rows, :] + pv
    m_sc[rows, :] = m_new


def _attn_finish(l_sc, acc_sc, g_ref, wuv_ref, o_ref, tq):
    for hd in range(MLA_HEADS):
        rows = slice(hd * tq, (hd + 1) * tq)
        cols = slice(hd * V_DIM, (hd + 1) * V_DIM)
        inv = 1.0 / jnp.sum(l_sc[rows, :], axis=-1, keepdims=True)
        o = _dot((acc_sc[rows, :] * inv).astype(BF16), wuv_ref[hd])
        o_ref[:, cols] = (o * g_ref[:, cols]).astype(o_ref.dtype)


def _attn_init(m_sc, l_sc, acc_sc):
    m_sc[...] = jnp.full(m_sc.shape, -jnp.inf, F32)
    l_sc[...] = jnp.zeros(l_sc.shape, F32)
    acc_sc[...] = jnp.zeros(acc_sc.shape, F32)


def _attn_scratch(rows, width):
    return [pltpu.VMEM((rows, LANES), F32), pltpu.VMEM((rows, LANES), F32), pltpu.VMEM((rows, width), F32)]


def _attn_prompt_kernel(q_ref, k_ref, v_ref, g_ref, o_ref, m_sc, l_sc, acc_sc, *s_sc, tq, tk):
    i = pl.program_id(1)
    _attn_init(m_sc, l_sc, acc_sc)
    span = 2 * tk
    lower = slice(span, tq)

    def issue_scores(bufs, p, rows):
        for half, s_ref in enumerate(bufs):
            keys = k_ref[pl.ds(pl.multiple_of(p * span + half * tk, tk), tk), :]
            s_ref[rows, :] = _scores(q_ref[rows, :], keys)

    def softmax(bufs, p, rows, own_pair=None):
        allowed = None
        if own_pair is not None:
            qchunk = (rows.start + lax.broadcasted_iota(jnp.int32, (rows.stop - rows.start, 1), 0)) // CHUNK
            kchunk = (own_pair * span + lax.broadcasted_iota(jnp.int32, (1, span), 1)) // CHUNK
            allowed = kchunk <= qchunk
        s = jnp.concatenate([s_ref[rows, :] for s_ref in bufs], axis=-1)
        _softmax_pv(s, v_ref[pl.ds(pl.multiple_of(p * span, span), span), :], m_sc, l_sc, acc_sc, allowed,
                    rows.start)

    sa, sb = s_sc[:2], s_sc[2:]
    every = slice(0, tq)
    issue_scores(sa, 0, every)

    def earlier_tile(t, carry):
        issue_scores(sb, 2 * t + 1, every)
        softmax(sa, 2 * t, every)
        issue_scores(sa, 2 * t + 2, every)
        softmax(sb, 2 * t + 1, every)
        return carry

    lax.fori_loop(0, i, earlier_tile, 0)
    issue_scores(sb, 2 * i + 1, lower)
    softmax(sa, 2 * i, every, own_pair=0)
    softmax(sb, 2 * i + 1, lower, own_pair=1)
    inv = 1.0 / jnp.sum(l_sc[...], axis=-1, keepdims=True)
    o_ref[...] = (acc_sc[...] * inv * g_ref[...]).astype(o_ref.dtype)


def _attn_prompt_call(qh, kh, vh, g, tq, tk):
    heads, rows, _ = qh.shape
    assert tk % CHUNK == 0 and rows % tq == 0 and tq == 4 * tk
    per_head = lambda n, m: pl.BlockSpec((None, n, m), lambda hd, i: (hd, 0, 0), pipeline_mode=pl.Buffered(1))
    return pl.pallas_call(
        functools.partial(_attn_prompt_kernel, tq=tq, tk=tk),
        grid=(heads, rows // tq),
        in_specs=[pl.BlockSpec((None, tq, QK_HEAD), lambda hd, i: (hd, i, 0)),
                  per_head(rows, QK_HEAD), per_head(rows, V_DIM),
                  pl.BlockSpec((tq, V_DIM), lambda hd, i: (i, hd))],
        out_specs=pl.BlockSpec((tq, V_DIM), lambda hd, i: (i, hd)),
        out_shape=jax.ShapeDtypeStruct((rows, heads * V_DIM), BF16),
        scratch_shapes=_attn_scratch(tq, V_DIM) + [pltpu.VMEM((tq, tk), F32)] * 4,
        compiler_params=pltpu.CompilerParams(dimension_semantics=("arbitrary", "arbitrary"),
                                             vmem_limit_bytes=VMEM_LIMIT),
        name="attn_prompt",
    )(qh, kh, vh, g)


def _attn_sample_kernel(q_ref, pckv_ref, pkpe_ref, knew_ref, g_ref, wuv_ref, o_ref, m_sc, l_sc, acc_sc, *, tq, tk):
    q = q_ref[...].reshape(MLA_HEADS * tq, QCAT)
    _attn_init(m_sc, l_sc, acc_sc)
    past = pckv_ref.shape[0]

    def body(j, carry):
        r0 = pl.multiple_of(j * tk, tk)
        kpe = pkpe_ref[pl.ds(r0, tk), :]
        kblk = jnp.concatenate([pckv_ref[pl.ds(r0, tk), :], kpe, kpe], axis=-1).astype(BF16)
        _softmax_pv(_scores(q, kblk), kblk[:, :KV_LORA], m_sc, l_sc, acc_sc, None)
        return carry

    lax.fori_loop(0, past // tk, body, 0)
    knew = knew_ref[...]
    _softmax_pv(_scores(q, knew), knew[:, :KV_LORA], m_sc, l_sc, acc_sc, None)
    _attn_finish(l_sc, acc_sc, g_ref, wuv_ref, o_ref, tq)


def _attn_sample_call(qcat, past_ckv, past_kpe, layer, kcat, g, wuv_t, batch, length, tk):
    past = past_ckv.shape[2]
    assert past % CHUNK == 0 and length <= CHUNK and past % tk == 0
    rows = batch * length
    r8 = MLA_HEADS * length
    return pl.pallas_call(
        functools.partial(_attn_sample_kernel, tq=length, tk=tk),
        grid=(batch,),
        in_specs=[pl.BlockSpec((MLA_HEADS, length, QCAT), lambda b: (0, b, 0)),
                  pl.BlockSpec((None, None, past, KV_LORA), lambda b: (layer, b, 0, 0)),
                  pl.BlockSpec((None, None, past, ROPE_DIM), lambda b: (layer, b, 0, 0)),
                  pl.BlockSpec((length, QCAT), lambda b: (b, 0)),
                  pl.BlockSpec((length, MLA_WIDTH), lambda b: (b, 0)),
                  _const_spec(wuv_t.shape)],
        out_specs=pl.BlockSpec((length, MLA_WIDTH), lambda b: (b, 0)),
        out_shape=jax.ShapeDtypeStruct((rows, MLA_WIDTH), BF16),
        scratch_shapes=_attn_scratch(r8, KV_LORA),
        compiler_params=pltpu.CompilerParams(dimension_semantics=("arbitrary",), vmem_limit_bytes=VMEM_LIMIT),
        name="attn_sample",
    )(qcat, past_ckv, past_kpe, kcat, g, wuv_t)


def _ssd_constants(cl):
    hb = LANES // cl
    ltri = np.tril(np.ones((cl, cl), np.float32))
    exp_s = np.zeros((LANES, SSD_HEADS * cl), np.float32)
    exp_p = np.zeros((LANES, SSD_WIDTH), np.float32)
    for hd in range(SSD_HEADS):
        exp_s[hd, hd * cl:(hd + 1) * cl] = 1.0
        exp_p[hd, hd * SSD_HEAD_DIM:(hd + 1) * SSD_HEAD_DIM] = 1.0
    t = np.arange(cl)[:, None]
    s = np.tile(np.arange(cl), SSD_HEADS)[None, :]
    eye = (t == s).astype(np.float32)
    tril = (t >= s).astype(np.float32)
    hrow = np.repeat(np.arange(hb), cl)[:, None]
    hcol = np.repeat(np.arange(hb), SSD_HEAD_DIM)[None, :]
    bdiag = (hrow == hcol).astype(np.float32)
    bf = lambda a: jnp.asarray(a, BF16)
    return (bf(np.tile(ltri, (1, CUM_PIECES))), bf(np.tile(exp_s, (CUM_PIECES, 1))),
            bf(np.tile(exp_p, (WIDE_PIECES, 1))), jnp.asarray(eye), jnp.asarray(tril), bf(bdiag))


def _split(x, n):
    pieces = []
    for _ in range(n):
        p = x.astype(BF16)
        pieces.append(p)
        x = x - p.astype(F32)
    return pieces


def _expand(x, sel, n):
    return _dot(jnp.concatenate(_split(x, n), axis=-1), sel)


def _ssd_kernel(act_ref, dt_ref, z_ref, h0_ref, alog_ref, dsk_ref, nw_ref,
                ltri_ref, exps_ref, expp_ref, eye_ref, tril_ref, bdiag_ref,
                y_ref, hout_ref, h_sc, wide_sc, e_sc, *, cl, cps):
    step = pl.program_id(1)
    last = pl.num_programs(1) - 1
    rows = cl * cps

    @pl.when(step == 0)
    def _():
        h_sc[...] = h0_ref[...]

    dt = dt_ref[...]
    a_pieces = _split(dt * -jnp.exp(alog_ref[...]), CUM_PIECES)
    cums, totals = [], []
    for c in range(cps):
        stacked = jnp.concatenate([p[c * cl:(c + 1) * cl, :] for p in a_pieces], axis=0)
        cum_c = _dot(ltri_ref[...], stacked)
        cums.append(cum_c)
        totals.append(jnp.broadcast_to(cum_c[cl - 1:cl, :], cum_c.shape))
    cum = jnp.concatenate(cums, axis=0)
    total = jnp.concatenate(totals, axis=0)
    wide_sc[...] = _expand(jnp.concatenate([dt, jnp.exp(total - cum), jnp.exp(cum)], axis=0), expp_ref[...],
                           WIDE_PIECES)
    e_sc[...] = _expand(cum, exps_ref[...], CUM_PIECES)

    hb = LANES // cl
    gs = GROUP_WIDTH
    gm = HEADS_PER_GROUP * cl

    def chunk(c, carry):
        r0 = pl.multiple_of(c * cl, cl)
        xs = act_ref[pl.ds(r0, cl), 0:SSD_WIDTH]
        bm = act_ref[pl.ds(r0, cl), SSD_WIDTH:SSD_WIDTH + SSD_GROUPS * SSD_STATE]
        cm = act_ref[pl.ds(r0, cl), SSD_WIDTH + SSD_GROUPS * SSD_STATE:CONV_DIM]
        dt_w = wide_sc[pl.ds(r0, cl), :]
        dstate_w = wide_sc[pl.ds(pl.multiple_of(rows + r0, cl), cl), :]
        ecum_w = wide_sc[pl.ds(pl.multiple_of(2 * rows + r0, cl), cl), :]
        e = e_sc[pl.ds(r0, cl), :]
        f_row = jnp.sum(jnp.where(eye_ref[...] > 0.0, e, 0.0), axis=0, keepdims=True)
        decay = jnp.where(tril_ref[...] > 0.0, jnp.exp(e - f_row), 0.0)
        xd = xs * dt_w
        w_state = (xd * dstate_w).astype(BF16)
        xd16 = xd.astype(BF16)
        h_prev = h_sc[...]
        h16 = h_prev.astype(BF16)
        bdiag = bdiag_ref[...]
        y_parts, st_parts = [], []
        for g in range(SSD_GROUPS):
            bg = bm[:, g * SSD_STATE:(g + 1) * SSD_STATE].astype(BF16)
            cg = cm[:, g * SSD_STATE:(g + 1) * SSD_STATE].astype(BF16)
            cb = lax.dot_general(cg, jnp.concatenate([bg] * hb, axis=0), (((1,), (1,)), ((), ())),
                                 preferred_element_type=F32)
            y_off = _dot(cg, h16[:, g * gs:(g + 1) * gs]) * ecum_w[:, g * gs:(g + 1) * gs]
            diag = []
            for j in range(gm // LANES):
                m = (cb * decay[:, g * gm + j * LANES:g * gm + (j + 1) * LANES]).astype(BF16)
                c0 = g * gs + j * hb * SSD_HEAD_DIM
                xj = xd16[:, c0:c0 + hb * SSD_HEAD_DIM]
                diag.append(_dot(m, jnp.concatenate([xj] * hb, axis=0) * bdiag))
            y_parts.append(y_off + jnp.concatenate(diag, axis=-1))
            st_parts.append(lax.dot_general(bg, w_state[:, g * gs:(g + 1) * gs], (((0,), (0,)), ((), ())),
                                            preferred_element_type=F32))
        h_sc[...] = h_prev * ecum_w[cl - 1:cl, :] + jnp.concatenate(st_parts, axis=-1)
        y = jnp.concatenate(y_parts, axis=-1) + xs * dsk_ref[...]
        y = y * z_ref[pl.ds(r0, cl), :]
        normed = [_rms(y[:, g * gs:(g + 1) * gs], nw_ref[:, g * gs:(g + 1) * gs]) for g in range(SSD_GROUPS)]
        y_ref[pl.ds(r0, cl), :] = jnp.concatenate(normed, axis=-1).astype(y_ref.dtype)
        return carry

    lax.fori_loop(0, cps, chunk, 0, unroll=True)

    @pl.when(step == last)
    def _():
        hout_ref[...] = h_sc[...]


def _ssd_call(act, dt, z, h0_t, state_layer, w, batch, length, cps):
    cl = min(CHUNK, length)
    rows = cl * cps
    assert length % rows == 0 and LANES % cl == 0
    consts = _ssd_constants(cl)
    seq = lambda n: pl.BlockSpec((None, rows, n), lambda b, s: (b, s, 0))
    params = (w["a_log"], w["d_skip"], w["ssd_norm_w"])
    in_specs = ([seq(CONV_DIM), seq(LANES), seq(SSD_WIDTH),
                 pl.BlockSpec((None, None, SSD_STATE, SSD_WIDTH), lambda b, s: (state_layer, b, 0, 0))]
                + [pl.BlockSpec(a.shape, lambda b, s: (0, 0)) for a in params + consts])
    out_shape = (jax.ShapeDtypeStruct((batch, length, SSD_WIDTH), BF16),
                 jax.ShapeDtypeStruct((batch, SSD_STATE, SSD_WIDTH), F32))
    out_specs = (seq(SSD_WIDTH), pl.BlockSpec((None, SSD_STATE, SSD_WIDTH), lambda b, s: (b, 0, 0)))
    return pl.pallas_call(
        functools.partial(_ssd_kernel, cl=cl, cps=cps),
        grid=(batch, length // rows), in_specs=in_specs, out_specs=out_specs, out_shape=out_shape,
        scratch_shapes=[pltpu.VMEM((SSD_STATE, SSD_WIDTH), F32), pltpu.VMEM((3 * rows, SSD_WIDTH), F32),
                        pltpu.VMEM((rows, SSD_HEADS * cl), F32)],
        compiler_params=pltpu.CompilerParams(dimension_semantics=("arbitrary", "arbitrary"),
                                             vmem_limit_bytes=VMEM_LIMIT),
        name="ssd",
    )(act.reshape(batch, length, CONV_DIM), dt.reshape(batch, length, LANES),
      z.reshape(batch, length, SSD_WIDTH), h0_t, *params, *consts)


def _post_kernel(mla_ref, ssd_ref, x_ref, wa_ref, wb_ref, fw_ref, o_ref, *, final):
    y = x_ref[...] + _dot(mla_ref[...], wa_ref[...]) + _dot(ssd_ref[...], wb_ref[...])
    o_ref[...] = _rms(y, fw_ref[...]) if final else y


def _post_call(mla, ssd, x, w, final_norm_w, final, tr):
    rows, d = x.shape
    row = lambda n: pl.BlockSpec((tr, n), lambda i: (i, 0))
    return pl.pallas_call(
        functools.partial(_post_kernel, final=final),
        grid=(rows // tr,),
        in_specs=[row(MLA_WIDTH), row(SSD_WIDTH), row(d), _const_spec(w["w_out_mla"].shape),
                  _const_spec(w["w_out_ssd"].shape), _const_spec(final_norm_w.shape)],
        out_specs=row(d), out_shape=jax.ShapeDtypeStruct((rows, d), F32),
        compiler_params=pltpu.CompilerParams(dimension_semantics=("arbitrary",), vmem_limit_bytes=VMEM_LIMIT),
        name="post",
    )(mla, ssd, x, w["w_out_mla"], w["w_out_ssd"], final_norm_w)


def _swap_halves(wcols):
    k, n = wcols.shape
    blocks = wcols.reshape(k, n // ROPE_DIM, 2, ROPE_DIM // 2)
    return blocks[:, :, ::-1, :].reshape(k, n)


def _layer_weights(i, norm_w, w_in, q_norm_w, w_uq, kv_norm_w, w_uk, w_uv, conv_w, conv_b, dt_bias, a_log, d_skip,
                   ssd_norm_w, w_out):
    d = w_in.shape[1]
    offs = np.cumsum((0, Q_LORA, KV_LORA, ROPE_DIM, MLA_WIDTH, SSD_WIDTH, CONV_DIM, SSD_HEADS))
    col = lambda j: w_in[i][:, offs[j]:offs[j + 1]]
    w_kpe = col(2)
    w_kpe_sw = _swap_halves(w_kpe)
    w_sm = jnp.concatenate([w_kpe, w_kpe, w_kpe_sw, w_kpe_sw], axis=1)
    w_qdt = jnp.concatenate([col(0), col(6), jnp.zeros((d, LANES - SSD_HEADS), F32)], axis=1)
    uq = w_uq[i].reshape(Q_LORA, MLA_HEADS, NOPE_DIM + ROPE_DIM)
    uq_pe = uq[:, :, NOPE_DIM:].reshape(Q_LORA, MLA_HEADS * ROPE_DIM)
    pad_heads = lambda v: jnp.concatenate([v, jnp.zeros((LANES - SSD_HEADS,), F32)])[None, :]
    return {
        "depth": w_in.shape[0],
        "norm_w": norm_w[i][None, :],
        "w_q": w_qdt.astype(BF16), "w_kv": col(1).astype(BF16), "w_g": col(3).astype(BF16),
        "w_z": col(4).astype(BF16), "w_xbc": col(5).astype(BF16), "w_sm": w_sm.astype(BF16),
        "q_norm_w": q_norm_w[i][None, :],
        "w_uq_nope": uq[:, :, :NOPE_DIM].reshape(Q_LORA, MLA_HEADS * NOPE_DIM).astype(BF16),
        "w_uq_pe": uq_pe.astype(BF16), "w_uq_pe_sw": _swap_halves(uq_pe).astype(BF16),
        "kv_norm_w": kv_norm_w[i][None, :],
        "w_uk_t": jnp.transpose(w_uk[i], (1, 2, 0)).astype(BF16),
        "w_uv_t": jnp.transpose(w_uv[i], (1, 0, 2)).astype(BF16),
        "w_uk_flat": w_uk[i].reshape(KV_LORA, MLA_HEADS * NOPE_DIM).astype(BF16),
        "w_uv_flat": w_uv[i].reshape(KV_LORA, MLA_HEADS * V_DIM).astype(BF16),
        "conv_w": conv_w[i], "conv_b": conv_b[i][None, :],
        "dt_bias": pad_heads(dt_bias[i]), "a_log": pad_heads(a_log[i]),
        "d_skip": jnp.repeat(d_skip[i], SSD_HEAD_DIM)[None, :],
        "ssd_norm_w": ssd_norm_w[i][None, :],
        "w_out_mla": w_out[i][:MLA_WIDTH].astype(BF16), "w_out_ssd": w_out[i][MLA_WIDTH:].astype(BF16),
    }


def _rope_tables(past, length, batch):
    half = ROPE_DIM // 2
    inv = 1.0 / (ROPE_THETA ** (np.arange(half, dtype=np.float64) / half))
    ang = (past + np.arange(length, dtype=np.float64))[:, None] * inv[None, :]
    cos, sin = np.cos(ang), np.sin(ang)
    reps = LANES // ROPE_DIM
    cos2 = np.tile(np.concatenate([cos, cos], axis=-1), (batch, reps))
    sin2 = np.tile(np.concatenate([-sin, sin], axis=-1), (batch, reps))
    return jnp.asarray(cos2, F32), jnp.asarray(sin2, F32)


def _state_to_lanes(h):
    b = h.shape[0]
    return jnp.transpose(h, (0, 3, 1, 2)).reshape(b, SSD_STATE, SSD_WIDTH)


def _state_from_lanes(ht):
    b = ht.shape[0]
    return jnp.transpose(ht.reshape(b, SSD_STATE, SSD_HEADS, SSD_HEAD_DIM), (0, 2, 3, 1))


def _pick(n, prefs):
    for p in prefs:
        if n % p == 0:
            return p
    return n


def _layer(x, batch, length, layer, state_layer, caches, conv_state, ssm_state_t, stacked, w, tables, final_norm_w):
    rows = batch * length
    tr = _pick(rows, (256, 128))
    absorbed = caches is not None
    *attn_in, ckv, kpe, g, z, act, dt, conv_new = _pre_call(x, conv_state, layer, state_layer, stacked, w, tables[0],
                                                            tables[1], batch, length, tr, absorbed)
    if absorbed:
        qcat, kcat = attn_in
        mla = _attn_sample_call(qcat, caches[0], caches[1], layer, kcat, g, w["w_uv_t"], batch, length,
                                _pick(caches[0].shape[2], (2048, 1024, 512, 256, 128, 64)))
    else:
        assert batch == 1
        tq = _pick(length, (2048, 1024, 512))
        mla = _attn_prompt_call(*attn_in, g, tq, min(tq // 4, 512))
    cl = min(CHUNK, length)
    ssd, h_t = _ssd_call(act, dt, z, ssm_state_t, state_layer, w, batch, length, _pick(length // cl, (4, 2, 1)))
    y = _post_call(mla, ssd.reshape(rows, SSD_WIDTH), x, w, final_norm_w, layer == w["depth"] - 1, tr)
    return y, (ckv, kpe), conv_new, h_t


def kernel(x_prompt, x_sample, cache_ckv, cache_kpe, state_conv, state_ssm, norm_w, w_in, q_norm_w, w_uq,
           kv_norm_w, w_uk, w_uv, conv_w, conv_b, dt_bias, a_log, d_skip, ssd_norm_w, w_out, final_norm_w):
    depth = w_in.shape[0]
    bp, lp, d = x_prompt.shape
    bs, ls, _ = x_sample.shape
    past = cache_ckv.shape[2]
    tab_p = _rope_tables(0, lp, bp)
    tab_s = _rope_tables(past, ls, bs)
    fw = final_norm_w[None, :]
    yp, ys = x_prompt.reshape(bp * lp, d), x_sample.reshape(bs * ls, d)
    zero_conv = jnp.zeros((1, bp, CONV_W - 1, CONV_DIM), F32)
    zero_ssm = jnp.zeros((1, bp, SSD_STATE, SSD_WIDTH), F32)
    ssm_t = _state_to_lanes(state_ssm.reshape((depth * bs,) + state_ssm.shape[2:]))
    ssm_t = ssm_t.reshape(depth, bs, SSD_STATE, SSD_WIDTH)
    kv_p = kv_s = None
    conv_p, conv_s, ssm_p, ssm_s = [], [], [], []
    for i in range(depth):
        w = _layer_weights(i, norm_w, w_in, q_norm_w, w_uq, kv_norm_w, w_uk, w_uv, conv_w, conv_b, dt_bias, a_log,
                           d_skip, ssd_norm_w, w_out)
        yp, kv_p, c_new, h_new = _layer(yp, bp, lp, i, 0, None, zero_conv, zero_ssm, kv_p, w, tab_p, fw)
        conv_p.append(c_new)
        ssm_p.append(_state_from_lanes(h_new))
        ys, kv_s, c_new, h_new = _layer(ys, bs, ls, i, i, (cache_ckv, cache_kpe), state_conv, ssm_t, kv_s, w,
                                        tab_s, fw)
        conv_s.append(c_new)
        ssm_s.append(_state_from_lanes(h_new))
    return (yp.reshape(bp, lp, d), ys.reshape(bs, ls, d),
            kv_p[0].reshape(depth, bp, lp, KV_LORA), kv_p[1].reshape(depth, bp, lp, ROPE_DIM),
            jnp.stack(conv_p), jnp.stack(ssm_p),
            kv_s[0].reshape(depth, bs, ls, KV_LORA), kv_s[1].reshape(depth, bs, ls, ROPE_DIM),
            jnp.stack(conv_s), jnp.stack(ssm_s))
```

```python
import functools

import numpy as np
import jax
import jax.numpy as jnp
from jax import lax
from jax.experimental import pallas as pl
from jax.experimental.pallas import tpu as pltpu

F32 = jnp.float32
BF16 = jnp.bfloat16

EPS = 1e-6
CHUNK = 64
MLA_HEADS = 8
Q_LORA = 384
KV_LORA = 256
NOPE_DIM = 128
ROPE_DIM = 64
V_DIM = 128
ROPE_THETA = 10000.0
ATTN_SCALE = (NOPE_DIM + ROPE_DIM) ** -0.5
Q_SCALE = ATTN_SCALE * float(np.log2(np.e))
SSD_HEADS = 16
SSD_HEAD_DIM = 64
SSD_GROUPS = 2
SSD_STATE = 128
CONV_W = 4
SSD_WIDTH = SSD_HEADS * SSD_HEAD_DIM
GROUP_WIDTH = SSD_WIDTH // SSD_GROUPS
HEADS_PER_GROUP = SSD_HEADS // SSD_GROUPS
CONV_DIM = SSD_WIDTH + 2 * SSD_GROUPS * SSD_STATE
MLA_WIDTH = MLA_HEADS * V_DIM
QCAT = KV_LORA + 2 * ROPE_DIM
QK_HEAD = NOPE_DIM + 2 * ROPE_DIM

CUM_PIECES = 3
WIDE_PIECES = 2

LANES = 128
SUBLANES = 8
VMEM_LIMIT = 56 * 1024 * 1024


def _dot(a, b):
    return jnp.dot(a, b, preferred_element_type=F32)


def _rms(x, w):
    return x * lax.rsqrt(jnp.mean(x * x, axis=-1, keepdims=True) + EPS) * w


def _silu(x):
    return x * (1.0 / (1.0 + jnp.exp(-x)))


def _const_spec(shape):
    nd = len(shape)
    return pl.BlockSpec(shape, lambda *_: (0,) * nd)


def _pre_kernel(*refs, seq_rows, tiles_per_seq, n_alias, absorbed):
    refs = refs[n_alias:]
    (x_ref, cst_ref, nw_ref, wq_ref, wkv_ref, wg_ref, wz_ref, wxbc_ref, wsm_ref, qnw_ref, wuqn_ref, wuqp_ref,
     wuqps_ref, kvnw_ref, cw_ref, cb_ref, dtb_ref) = refs[:17]
    n_w, n_out = (1, 2) if absorbed else (2, 3)
    attn_w = refs[17:17 + n_w]
    cos_ref, sin_ref = refs[17 + n_w:19 + n_w]
    attn_out = refs[19 + n_w:19 + n_w + n_out]
    ckv_ref, kpe_ref, g_ref, z_ref, xbc_ref, dt_ref, cnew_ref, buf_sc = refs[19 + n_w + n_out:]
    keep = CONV_W - 1
    base = SUBLANES - keep
    if tiles_per_seq > 1:
        @pl.when(pl.program_id(0) % tiles_per_seq == 0)
        def _():
            buf_sc[0:base, :] = jnp.zeros((base, buf_sc.shape[1]), F32)
            buf_sc[base:SUBLANES, :] = cst_ref[0]

    x = x_ref[...]
    h = _rms(x, nw_ref[...]).astype(BF16)
    xbc = _dot(h, wxbc_ref[...])
    g_ref[...] = _silu(_dot(h, wg_ref[...]))
    z_ref[...] = _silu(_dot(h, wz_ref[...]))

    cos2 = cos_ref[...]
    sin2 = sin_ref[...]
    sm = _dot(h, wsm_ref[...])
    kpe2 = sm[:, 0:LANES] * cos2 + sm[:, LANES:2 * LANES] * sin2
    kpe_ref[...] = kpe2[:, :ROPE_DIM]
    ckv = _rms(_dot(h, wkv_ref[...]), kvnw_ref[...])
    ckv_ref[...] = ckv

    qdt = _dot(h, wq_ref[...])
    dt_in = qdt[:, Q_LORA:Q_LORA + LANES] + dtb_ref[...]
    dt_ref[...] = jnp.maximum(dt_in, 0.0) + jnp.log1p(jnp.exp(-jnp.abs(dt_in)))
    qn = _rms(qdt[:, :Q_LORA], qnw_ref[...]).astype(BF16)
    qnope = _dot(qn, wuqn_ref[...])
    pairs = MLA_HEADS // 2
    cos_all = jnp.concatenate([cos2] * pairs, axis=-1)
    sin_all = jnp.concatenate([sin2] * pairs, axis=-1)
    qpe = _dot(qn, wuqp_ref[...]) * cos_all + _dot(qn, wuqps_ref[...]) * sin_all
    lane = lax.broadcasted_iota(jnp.int32, (x.shape[0], LANES), 1)
    ckv16 = ckv.astype(BF16)
    if absorbed:
        (wuk_ref,) = attn_w
        q_ref, kcat_ref = attn_out
        kcat_ref[...] = jnp.concatenate([ckv16, kpe2.astype(BF16)], axis=-1)
    else:
        wukf_ref, wuvf_ref = attn_w
        q_ref, kh_ref, vh_ref = attn_out
        k_nope = _dot(ckv16, wukf_ref[...])
        v_all = _dot(ckv16, wuvf_ref[...])
    for hd in range(MLA_HEADS):
        cols = slice(hd * NOPE_DIM, (hd + 1) * NOPE_DIM)
        pair = qpe[:, (hd // 2) * LANES:(hd // 2 + 1) * LANES] * Q_SCALE
        own_half = (lane < ROPE_DIM) if hd % 2 == 0 else (lane >= ROPE_DIM)
        q_pe = jnp.where(own_half, pair, 0.0)
        if absorbed:
            q_main = _dot(qnope[:, cols].astype(BF16), wuk_ref[hd]) * Q_SCALE
        else:
            q_main = qnope[:, cols] * Q_SCALE
            kh_ref[hd] = jnp.concatenate([k_nope[:, cols], kpe2], axis=-1).astype(BF16)
            vh_ref[hd] = v_all[:, hd * V_DIM:(hd + 1) * V_DIM].astype(BF16)
        q_ref[hd] = jnp.concatenate([q_main, q_pe], axis=-1).astype(BF16)

    for s in range(x.shape[0] // seq_rows):
        if tiles_per_seq == 1:
            buf_sc[0:base, :] = jnp.zeros((base, buf_sc.shape[1]), F32)
            buf_sc[base:SUBLANES, :] = cst_ref[s]
        buf_sc[SUBLANES:SUBLANES + seq_rows, :] = xbc[s * seq_rows:(s + 1) * seq_rows, :]
        full = buf_sc[...]
        conv = cb_ref[...] + cw_ref[keep:CONV_W, :] * full[SUBLANES:, :]
        for k in range(1, CONV_W):
            shifted = pltpu.roll(full, k, axis=0)[SUBLANES:, :]
            conv = conv + cw_ref[keep - k:CONV_W - k, :] * shifted
        xbc_ref[s * seq_rows:(s + 1) * seq_rows, :] = _silu(conv)
        tail = buf_sc[base + seq_rows:SUBLANES + seq_rows, :]
        buf_sc[base:SUBLANES, :] = tail
        cnew_ref[s] = tail


def _pre_call(x, conv_state, layer, state_layer, stacked, w, cos2, sin2, batch, length, tr, absorbed):
    rows, d = x.shape
    depth = w["depth"]
    seq_rows = min(length, tr)
    tiles_per_seq = length // seq_rows
    nseq = tr // seq_rows
    assert rows == batch * length and length % seq_rows == 0 and tr % seq_rows == 0 and seq_rows >= CONV_W - 1
    row = lambda n: pl.BlockSpec((tr, n), lambda i: (i, 0))
    layer_row = lambda n: pl.BlockSpec((None, tr, n), lambda i: (layer, i, 0))
    heads = lambda n: pl.BlockSpec((MLA_HEADS, tr, n), lambda i: (0, i, 0))
    head_shape = lambda n: jax.ShapeDtypeStruct((MLA_HEADS, rows, n), BF16)
    if absorbed:
        attn_w = (w["w_uk_t"],)
        attn_shape = (head_shape(QCAT), jax.ShapeDtypeStruct((rows, QCAT), BF16))
        attn_specs = (heads(QCAT), row(QCAT))
    else:
        attn_w = (w["w_uk_flat"], w["w_uv_flat"])
        attn_shape = (head_shape(QK_HEAD), head_shape(QK_HEAD), head_shape(V_DIM))
        attn_specs = (heads(QK_HEAD), heads(QK_HEAD), heads(V_DIM))
    weights = (w["norm_w"], w["w_q"], w["w_kv"], w["w_g"], w["w_z"], w["w_xbc"], w["w_sm"],
               w["q_norm_w"], w["w_uq_nope"], w["w_uq_pe"], w["w_uq_pe_sw"], w["kv_norm_w"],
               w["conv_w"], w["conv_b"], w["dt_bias"]) + attn_w
    aliased = tuple(stacked)
    in_specs = ([pl.BlockSpec(memory_space=pl.ANY)] * len(aliased)
                + [row(d), pl.BlockSpec((None, nseq, CONV_W - 1, CONV_DIM),
                                        lambda i: (state_layer, i // tiles_per_seq, 0, 0))]
                + [_const_spec(a.shape) for a in weights] + [row(LANES), row(LANES)])
    out_shape = attn_shape + (
        jax.ShapeDtypeStruct((depth, rows, KV_LORA), F32),
        jax.ShapeDtypeStruct((depth, rows, ROPE_DIM), F32),
        jax.ShapeDtypeStruct((rows, MLA_WIDTH), F32),
        jax.ShapeDtypeStruct((rows, SSD_WIDTH), F32),
        jax.ShapeDtypeStruct((rows, CONV_DIM), F32),
        jax.ShapeDtypeStruct((rows, LANES), F32),
        jax.ShapeDtypeStruct((batch, CONV_W - 1, CONV_DIM), F32),
    )
    out_specs = attn_specs + (
        layer_row(KV_LORA), layer_row(ROPE_DIM), row(MLA_WIDTH), row(SSD_WIDTH), row(CONV_DIM),
        row(LANES), pl.BlockSpec((nseq, CONV_W - 1, CONV_DIM), lambda i: (i // tiles_per_seq, 0, 0)),
    )
    n_attn = len(attn_shape)
    return pl.pallas_call(
        functools.partial(_pre_kernel, seq_rows=seq_rows, tiles_per_seq=tiles_per_seq, n_alias=len(aliased),
                          absorbed=absorbed),
        grid=(rows // tr,), in_specs=in_specs, out_specs=out_specs, out_shape=out_shape,
        scratch_shapes=[pltpu.VMEM((seq_rows + SUBLANES, CONV_DIM), F32)],
        input_output_aliases={k: n_attn + k for k in range(len(aliased))},
        compiler_params=pltpu.CompilerParams(dimension_semantics=("arbitrary",), vmem_limit_bytes=VMEM_LIMIT),
        name="pre",
    )(*aliased, x, conv_state, *weights, cos2, sin2)


def _scores(q, kblk):
    return lax.dot_general(q, kblk, (((1,), (1,)), ((), ())), preferred_element_type=F32)


def _softmax_pv(s, v, m_sc, l_sc, acc_sc, allowed, row0=0):
    rows = slice(row0, row0 + s.shape[0])
    if allowed is not None:
        s = jnp.where(allowed, s, -jnp.inf)
    tk = s.shape[1]
    m_prev = m_sc[rows, :]
    if tk % LANES == 0:
        chunks = [s[:, c * LANES:(c + 1) * LANES] for c in range(tk // LANES)]
        m_new = jnp.maximum(m_prev, jnp.max(functools.reduce(jnp.maximum, chunks), axis=-1, keepdims=True))
        ps = [jnp.exp2(c - m_new) for c in chunks]
        psum = functools.reduce(jnp.add, ps)
        p = jnp.concatenate(ps, axis=-1)
    else:
        m_new = jnp.maximum(m_prev, jnp.max(s, axis=-1, keepdims=True))
        p = jnp.exp2(s - m_new[:, :1])
        psum = jnp.sum(p, axis=-1, keepdims=True) * (1.0 / LANES)
    alpha = jnp.exp2(m_prev - m_new)
    l_sc[rows, :] = alpha * l_sc[rows, :] + psum
    pv = _dot(p.astype(BF16), v)
    acc_sc[rows, :] = jnp.concatenate([alpha] * (acc_sc.shape[1] // LANES), axis=-1) * acc_sc[rows, :] + pv
    m_sc[rows, :] = m_new


def _attn_finish(l_sc, acc_sc, g_ref, wuv_ref, o_ref, tq):
    for hd in range(MLA_HEADS):
        rows = slice(hd * tq, (hd + 1) * tq)
        cols = slice(hd * V_DIM, (hd + 1) * V_DIM)
        inv = 1.0 / jnp.sum(l_sc[rows, :], axis=-1, keepdims=True)
        o = _dot((acc_sc[rows, :] * inv).astype(BF16), wuv_ref[hd])
        o_ref[:, cols] = (o * g_ref[:, cols]).astype(o_ref.dtype)


def _attn_init(m_sc, l_sc, acc_sc):
    m_sc[...] = jnp.full(m_sc.shape, -jnp.inf, F32)
    l_sc[...] = jnp.zeros(l_sc.shape, F32)
    acc_sc[...] = jnp.zeros(acc_sc.shape, F32)


def _attn_scratch(rows, width):
    return [pltpu.VMEM((rows, LANES), F32), pltpu.VMEM((rows, LANES), F32), pltpu.VMEM((rows, width), F32)]


def _attn_prompt_kernel(q_ref, k_ref, v_ref, g_ref, o_ref, m_sc, l_sc, acc_sc, *s_sc, tq, tk):
    i = pl.program_id(1)
    _attn_init(m_sc, l_sc, acc_sc)
    span = 2 * tk
    lower = slice(span, tq)

    def issue_scores(bufs, p, rows):
        for half, s_ref in enumerate(bufs):
            keys = k_ref[pl.ds(pl.multiple_of(p * span + half * tk, tk), tk), :]
            s_ref[rows, :] = _scores(q_ref[rows, :], keys)

    def softmax(bufs, p, rows, own_pair=None):
        allowed = None
        if own_pair is not None:
            qchunk = (rows.start + lax.broadcasted_iota(jnp.int32, (rows.stop - rows.start, 1), 0)) // CHUNK
            kchunk = (own_pair * span + lax.broadcasted_iota(jnp.int32, (1, span), 1)) // CHUNK
            allowed = kchunk <= qchunk
        s = jnp.concatenate([s_ref[rows, :] for s_ref in bufs], axis=-1)
        _softmax_pv(s, v_ref[pl.ds(pl.multiple_of(p * span, span), span), :], m_sc, l_sc, acc_sc, allowed,
                    rows.start)

    sa, sb = s_sc[:2], s_sc[2:]
    every = slice(0, tq)
    issue_scores(sa, 0, every)

    def earlier_tile(t, carry):
        issue_scores(sb, 2 * t + 1, every)
        softmax(sa, 2 * t, every)
        issue_scores(sa, 2 * t + 2, every)
        softmax(sb, 2 * t + 1, every)
        return carry

    lax.fori_loop(0, i, earlier_tile, 0)
    issue_scores(sb, 2 * i + 1, lower)
    softmax(sa, 2 * i, every, own_pair=0)
    softmax(sb, 2 * i + 1, lower, own_pair=1)
    inv = 1.0 / jnp.sum(l_sc[...], axis=-1, keepdims=True)
    o_ref[...] = (acc_sc[...] * inv * g_ref[...]).astype(o_ref.dtype)


def _attn_prompt_call(qh, kh, vh, g, tq, tk):
    heads, rows, _ = qh.shape
    assert tk % CHUNK == 0 and rows % tq == 0 and tq == 4 * tk
    per_head = lambda n, m: pl.BlockSpec((None, n, m), lambda hd, i: (hd, 0, 0), pipeline_mode=pl.Buffered(1))
    return pl.pallas_call(
        functools.partial(_attn_prompt_kernel, tq=tq, tk=tk),
        grid=(heads, rows // tq),
        in_specs=[pl.BlockSpec((None, tq, QK_HEAD), lambda hd, i: (hd, i, 0)),
                  per_head(rows, QK_HEAD), per_head(rows, V_DIM),
                  pl.BlockSpec((tq, V_DIM), lambda hd, i: (i, hd))],
        out_specs=pl.BlockSpec((tq, V_DIM), lambda hd, i: (i, hd)),
        out_shape=jax.ShapeDtypeStruct((rows, heads * V_DIM), BF16),
        scratch_shapes=_attn_scratch(tq, V_DIM) + [pltpu.VMEM((tq, tk), F32)] * 4,
        compiler_params=pltpu.CompilerParams(dimension_semantics=("arbitrary", "arbitrary"),
                                             vmem_limit_bytes=VMEM_LIMIT),
        name="attn_prompt",
    )(qh, kh, vh, g)


def _attn_sample_kernel(q_ref, pckv_ref, pkpet_ref, knew_ref, g_ref, wuv_ref, o_ref, m_sc, l_sc, acc_sc, *, tq, tk):
    q = q_ref[...].reshape(MLA_HEADS * tq, QCAT)
    _attn_init(m_sc, l_sc, acc_sc)
    past = pckv_ref.shape[0]
    q_lat = q[:, :KV_LORA]
    pair = q[:, KV_LORA:]
    q_pe = pair[:, :ROPE_DIM] + pair[:, ROPE_DIM:]

    def body(j, carry):
        r0 = pl.multiple_of(j * tk, tk)
        ckv = pckv_ref[pl.ds(r0, tk), :].astype(BF16)
        s = _scores(q_lat, ckv) + _dot(q_pe, pkpet_ref[:, pl.ds(r0, tk)].astype(BF16))
        _softmax_pv(s, ckv, m_sc, l_sc, acc_sc, None)
        return carry

    lax.fori_loop(0, past // tk, body, 0)
    knew = knew_ref[...]
    _softmax_pv(_scores(q, knew), knew[:, :KV_LORA], m_sc, l_sc, acc_sc, None)
    _attn_finish(l_sc, acc_sc, g_ref, wuv_ref, o_ref, tq)


def _attn_sample_call(qcat, past_ckv, past_kpe_t, layer, kcat, g, wuv_t, batch, length, tk):
    past = past_ckv.shape[2]
    assert past % CHUNK == 0 and length <= CHUNK and past % tk == 0
    rows = batch * length
    r8 = MLA_HEADS * length
    return pl.pallas_call(
        functools.partial(_attn_sample_kernel, tq=length, tk=tk),
        grid=(batch,),
        in_specs=[pl.BlockSpec((MLA_HEADS, length, QCAT), lambda b: (0, b, 0)),
                  pl.BlockSpec((None, None, past, KV_LORA), lambda b: (layer, b, 0, 0)),
                  pl.BlockSpec((None, None, ROPE_DIM, past), lambda b: (layer, b, 0, 0)),
                  pl.BlockSpec((length, QCAT), lambda b: (b, 0)),
                  pl.BlockSpec((length, MLA_WIDTH), lambda b: (b, 0)),
                  _const_spec(wuv_t.shape)],
        out_specs=pl.BlockSpec((length, MLA_WIDTH), lambda b: (b, 0)),
        out_shape=jax.ShapeDtypeStruct((rows, MLA_WIDTH), BF16),
        scratch_shapes=_attn_scratch(r8, KV_LORA),
        compiler_params=pltpu.CompilerParams(dimension_semantics=("arbitrary",), vmem_limit_bytes=VMEM_LIMIT),
        name="attn_sample",
    )(qcat, past_ckv, past_kpe_t, kcat, g, wuv_t)


def _ssd_constants(cl):
    hb = LANES // cl
    ltri = np.tril(np.ones((cl, cl), np.float32))
    exp_s = np.zeros((LANES, SSD_HEADS * cl), np.float32)
    exp_p = np.zeros((LANES, SSD_WIDTH), np.float32)
    for hd in range(SSD_HEADS):
        exp_s[hd, hd * cl:(hd + 1) * cl] = 1.0
        exp_p[hd, hd * SSD_HEAD_DIM:(hd + 1) * SSD_HEAD_DIM] = 1.0
    t = np.arange(cl)[:, None]
    s = np.tile(np.arange(cl), SSD_HEADS)[None, :]
    eye = (t == s).astype(np.float32)
    tril = (t >= s).astype(np.float32)
    hrow = np.repeat(np.arange(hb), cl)[:, None]
    hcol = np.repeat(np.arange(hb), SSD_HEAD_DIM)[None, :]
    bdiag = (hrow == hcol).astype(np.float32)
    bf = lambda a: jnp.asarray(a, BF16)
    return (bf(np.tile(ltri, (1, CUM_PIECES))), bf(np.tile(exp_s, (CUM_PIECES, 1))),
            bf(np.tile(exp_p, (WIDE_PIECES, 1))), jnp.asarray(eye), jnp.asarray(tril), bf(bdiag))


def _split(x, n):
    pieces = []
    for _ in range(n):
        p = x.astype(BF16)
        pieces.append(p)
        x = x - p.astype(F32)
    return pieces


def _expand(x, sel, n):
    return _dot(jnp.concatenate(_split(x, n), axis=-1), sel)


def _ssd_kernel(act_ref, dt_ref, z_ref, h0_ref, alog_ref, dsk_ref, nw_ref,
                ltri_ref, exps_ref, expp_ref, eye_ref, tril_ref, bdiag_ref,
                y_ref, hout_ref, h_sc, wide_sc, e_sc, *, cl, cps):
    step = pl.program_id(1)
    last = pl.num_programs(1) - 1
    rows = cl * cps

    @pl.when(step == 0)
    def _():
        h_sc[...] = h0_ref[...]

    dt = dt_ref[...]
    a_pieces = _split(dt * -jnp.exp(alog_ref[...]), CUM_PIECES)
    cums, totals = [], []
    for c in range(cps):
        stacked = jnp.concatenate([p[c * cl:(c + 1) * cl, :] for p in a_pieces], axis=0)
        cum_c = _dot(ltri_ref[...], stacked)
        cums.append(cum_c)
        totals.append(jnp.broadcast_to(cum_c[cl - 1:cl, :], cum_c.shape))
    cum = jnp.concatenate(cums, axis=0)
    total = jnp.concatenate(totals, axis=0)
    wide_sc[...] = _expand(jnp.concatenate([dt, jnp.exp(total - cum), jnp.exp(cum)], axis=0), expp_ref[...],
                           WIDE_PIECES)
    e_sc[...] = _expand(cum, exps_ref[...], CUM_PIECES)

    hb = LANES // cl
    gs = GROUP_WIDTH
    gm = HEADS_PER_GROUP * cl

    def chunk(c, carry):
        r0 = pl.multiple_of(c * cl, cl)
        xs = act_ref[pl.ds(r0, cl), 0:SSD_WIDTH]
        bm = act_ref[pl.ds(r0, cl), SSD_WIDTH:SSD_WIDTH + SSD_GROUPS * SSD_STATE]
        cm = act_ref[pl.ds(r0, cl), SSD_WIDTH + SSD_GROUPS * SSD_STATE:CONV_DIM]
        dt_w = wide_sc[pl.ds(r0, cl), :]
        dstate_w = wide_sc[pl.ds(pl.multiple_of(rows + r0, cl), cl), :]
        ecum_w = wide_sc[pl.ds(pl.multiple_of(2 * rows + r0, cl), cl), :]
        e = e_sc[pl.ds(r0, cl), :]
        f_row = jnp.sum(jnp.where(eye_ref[...] > 0.0, e, 0.0), axis=0, keepdims=True)
        decay = jnp.where(tril_ref[...] > 0.0, jnp.exp(e - f_row), 0.0)
        xd = xs * dt_w
        w_state = (xd * dstate_w).astype(BF16)
        xd16 = xd.astype(BF16)
        h_prev = h_sc[...]
        h16 = h_prev.astype(BF16)
        bdiag = bdiag_ref[...]
        y_parts, st_parts = [], []
        for g in range(SSD_GROUPS):
            bg = bm[:, g * SSD_STATE:(g + 1) * SSD_STATE].astype(BF16)
            cg = cm[:, g * SSD_STATE:(g + 1) * SSD_STATE].astype(BF16)
            cb = lax.dot_general(cg, jnp.concatenate([bg] * hb, axis=0), (((1,), (1,)), ((), ())),
                                 preferred_element_type=F32)
            y_off = _dot(cg, h16[:, g * gs:(g + 1) * gs]) * ecum_w[:, g * gs:(g + 1) * gs]
            diag = []
            for j in range(gm // LANES):
                m = (cb * decay[:, g * gm + j * LANES:g * gm + (j + 1) * LANES]).astype(BF16)
                c0 = g * gs + j * hb * SSD_HEAD_DIM
                xj = xd16[:, c0:c0 + hb * SSD_HEAD_DIM]
                diag.append(_dot(m, jnp.concatenate([xj] * hb, axis=0) * bdiag))
            y_parts.append(y_off + jnp.concatenate(diag, axis=-1))
            st_parts.append(lax.dot_general(bg, w_state[:, g * gs:(g + 1) * gs], (((0,), (0,)), ((), ())),
                                            preferred_element_type=F32))
        h_sc[...] = h_prev * ecum_w[cl - 1:cl, :] + jnp.concatenate(st_parts, axis=-1)
        y = jnp.concatenate(y_parts, axis=-1) + xs * dsk_ref[...]
        y = y * z_ref[pl.ds(r0, cl), :]
        normed = [_rms(y[:, g * gs:(g + 1) * gs], nw_ref[:, g * gs:(g + 1) * gs]) for g in range(SSD_GROUPS)]
        y_ref[pl.ds(r0, cl), :] = jnp.concatenate(normed, axis=-1).astype(y_ref.dtype)
        return carry

    lax.fori_loop(0, cps, chunk, 0, unroll=True)

    @pl.when(step == last)
    def _():
        hout_ref[...] = h_sc[...]


def _ssd_call(act, dt, z, h0_t, state_layer, w, batch, length, cps):
    cl = min(CHUNK, length)
    rows = cl * cps
    assert length % rows == 0 and LANES % cl == 0
    consts = _ssd_constants(cl)
    seq = lambda n: pl.BlockSpec((None, rows, n), lambda b, s: (b, s, 0))
    params = (w["a_log"], w["d_skip"], w["ssd_norm_w"])
    in_specs = ([seq(CONV_DIM), seq(LANES), seq(SSD_WIDTH),
                 pl.BlockSpec((None, None, SSD_STATE, SSD_WIDTH), lambda b, s: (state_layer, b, 0, 0))]
                + [pl.BlockSpec(a.shape, lambda b, s: (0, 0)) for a in params + consts])
    out_shape = (jax.ShapeDtypeStruct((batch, length, SSD_WIDTH), BF16),
                 jax.ShapeDtypeStruct((batch, SSD_STATE, SSD_WIDTH), F32))
    out_specs = (seq(SSD_WIDTH), pl.BlockSpec((None, SSD_STATE, SSD_WIDTH), lambda b, s: (b, 0, 0)))
    return pl.pallas_call(
        functools.partial(_ssd_kernel, cl=cl, cps=cps),
        grid=(batch, length // rows), in_specs=in_specs, out_specs=out_specs, out_shape=out_shape,
        scratch_shapes=[pltpu.VMEM((SSD_STATE, SSD_WIDTH), F32), pltpu.VMEM((3 * rows, SSD_WIDTH), F32),
                        pltpu.VMEM((rows, SSD_HEADS * cl), F32)],
        compiler_params=pltpu.CompilerParams(dimension_semantics=("arbitrary", "arbitrary"),
                                             vmem_limit_bytes=VMEM_LIMIT),
        name="ssd",
    )(act.reshape(batch, length, CONV_DIM), dt.reshape(batch, length, LANES),
      z.reshape(batch, length, SSD_WIDTH), h0_t, *params, *consts)


def _post_kernel(mla_ref, ssd_ref, x_ref, wa_ref, wb_ref, fw_ref, o_ref, *, final):
    y = x_ref[...] + _dot(mla_ref[...], wa_ref[...]) + _dot(ssd_ref[...], wb_ref[...])
    o_ref[...] = _rms(y, fw_ref[...]) if final else y


def _post_call(mla, ssd, x, w, final_norm_w, final, tr):
    rows, d = x.shape
    row = lambda n: pl.BlockSpec((tr, n), lambda i: (i, 0))
    return pl.pallas_call(
        functools.partial(_post_kernel, final=final),
        grid=(rows // tr,),
        in_specs=[row(MLA_WIDTH), row(SSD_WIDTH), row(d), _const_spec(w["w_out_mla"].shape),
                  _const_spec(w["w_out_ssd"].shape), _const_spec(final_norm_w.shape)],
        out_specs=row(d), out_shape=jax.ShapeDtypeStruct((rows, d), F32),
        compiler_params=pltpu.CompilerParams(dimension_semantics=("arbitrary",), vmem_limit_bytes=VMEM_LIMIT),
        name="post",
    )(mla, ssd, x, w["w_out_mla"], w["w_out_ssd"], final_norm_w)


def _swap_halves(wcols):
    k, n = wcols.shape
    blocks = wcols.reshape(k, n // ROPE_DIM, 2, ROPE_DIM // 2)
    return blocks[:, :, ::-1, :].reshape(k, n)


def _layer_weights(i, norm_w, w_in, q_norm_w, w_uq, kv_norm_w, w_uk, w_uv, conv_w, conv_b, dt_bias, a_log, d_skip,
                   ssd_norm_w, w_out):
    d = w_in.shape[1]
    offs = np.cumsum((0, Q_LORA, KV_LORA, ROPE_DIM, MLA_WIDTH, SSD_WIDTH, CONV_DIM, SSD_HEADS))
    col = lambda j: w_in[i][:, offs[j]:offs[j + 1]]
    w_kpe = col(2)
    w_kpe_sw = _swap_halves(w_kpe)
    w_sm = jnp.concatenate([w_kpe, w_kpe, w_kpe_sw, w_kpe_sw], axis=1)
    w_qdt = jnp.concatenate([col(0), col(6), jnp.zeros((d, LANES - SSD_HEADS), F32)], axis=1)
    uq = w_uq[i].reshape(Q_LORA, MLA_HEADS, NOPE_DIM + ROPE_DIM)
    uq_pe = uq[:, :, NOPE_DIM:].reshape(Q_LORA, MLA_HEADS * ROPE_DIM)
    pad_heads = lambda v: jnp.concatenate([v, jnp.zeros((LANES - SSD_HEADS,), F32)])[None, :]
    return {
        "depth": w_in.shape[0],
        "norm_w": norm_w[i][None, :],
        "w_q": w_qdt.astype(BF16), "w_kv": col(1).astype(BF16), "w_g": col(3).astype(BF16),
        "w_z": col(4).astype(BF16), "w_xbc": col(5).astype(BF16), "w_sm": w_sm.astype(BF16),
        "q_norm_w": q_norm_w[i][None, :],
        "w_uq_nope": uq[:, :, :NOPE_DIM].reshape(Q_LORA, MLA_HEADS * NOPE_DIM).astype(BF16),
        "w_uq_pe": uq_pe.astype(BF16), "w_uq_pe_sw": _swap_halves(uq_pe).astype(BF16),
        "kv_norm_w": kv_norm_w[i][None, :],
        "w_uk_t": jnp.transpose(w_uk[i], (1, 2, 0)).astype(BF16),
        "w_uv_t": jnp.transpose(w_uv[i], (1, 0, 2)).astype(BF16),
        "w_uk_flat": w_uk[i].reshape(KV_LORA, MLA_HEADS * NOPE_DIM).astype(BF16),
        "w_uv_flat": w_uv[i].reshape(KV_LORA, MLA_HEADS * V_DIM).astype(BF16),
        "conv_w": conv_w[i], "conv_b": conv_b[i][None, :],
        "dt_bias": pad_heads(dt_bias[i]), "a_log": pad_heads(a_log[i]),
        "d_skip": jnp.repeat(d_skip[i], SSD_HEAD_DIM)[None, :],
        "ssd_norm_w": ssd_norm_w[i][None, :],
        "w_out_mla": w_out[i][:MLA_WIDTH].astype(BF16), "w_out_ssd": w_out[i][MLA_WIDTH:].astype(BF16),
    }


def _rope_tables(past, length, batch):
    half = ROPE_DIM // 2
    inv = 1.0 / (ROPE_THETA ** (np.arange(half, dtype=np.float64) / half))
    ang = (past + np.arange(length, dtype=np.float64))[:, None] * inv[None, :]
    cos, sin = np.cos(ang), np.sin(ang)
    reps = LANES // ROPE_DIM
    cos2 = np.tile(np.concatenate([cos, cos], axis=-1), (batch, reps))
    sin2 = np.tile(np.concatenate([-sin, sin], axis=-1), (batch, reps))
    return jnp.asarray(cos2, F32), jnp.asarray(sin2, F32)


def _state_to_lanes(h):
    b = h.shape[0]
    return jnp.transpose(h, (0, 3, 1, 2)).reshape(b, SSD_STATE, SSD_WIDTH)


def _state_from_lanes(ht):
    b = ht.shape[0]
    return jnp.transpose(ht.reshape(b, SSD_STATE, SSD_HEADS, SSD_HEAD_DIM), (0, 2, 3, 1))


def _pick(n, prefs):
    for p in prefs:
        if n % p == 0:
            return p
    return n


def _layer(x, batch, length, layer, state_layer, caches, conv_state, ssm_state_t, stacked, w, tables, final_norm_w):
    rows = batch * length
    tr = _pick(rows, (256, 128))
    absorbed = caches is not None
    *attn_in, ckv, kpe, g, z, act, dt, conv_new = _pre_call(x, conv_state, layer, state_layer, stacked, w, tables[0],
                                                            tables[1], batch, length, tr, absorbed)
    if absorbed:
        qcat, kcat = attn_in
        mla = _attn_sample_call(qcat, caches[0], caches[1], layer, kcat, g, w["w_uv_t"], batch, length,
                                _pick(caches[0].shape[2], (2048, 1024, 512, 256, 128, 64)))
    else:
        assert batch == 1
        tq = _pick(length, (2048, 1024, 512))
        mla = _attn_prompt_call(*attn_in, g, tq, min(tq // 4, 512))
    cl = min(CHUNK, length)
    ssd, h_t = _ssd_call(act, dt, z, ssm_state_t, state_layer, w, batch, length, _pick(length // cl, (4, 2, 1)))
    y = _post_call(mla, ssd.reshape(rows, SSD_WIDTH), x, w, final_norm_w, layer == w["depth"] - 1, tr)
    return y, (ckv, kpe), conv_new, h_t


def kernel(x_prompt, x_sample, cache_ckv, cache_kpe, state_conv, state_ssm, norm_w, w_in, q_norm_w, w_uq,
           kv_norm_w, w_uk, w_uv, conv_w, conv_b, dt_bias, a_log, d_skip, ssd_norm_w, w_out, final_norm_w):
    depth = w_in.shape[0]
    bp, lp, d = x_prompt.shape
    bs, ls, _ = x_sample.shape
    past = cache_ckv.shape[2]
    tab_p = _rope_tables(0, lp, bp)
    tab_s = _rope_tables(past, ls, bs)
    fw = final_norm_w[None, :]
    yp, ys = x_prompt.reshape(bp * lp, d), x_sample.reshape(bs * ls, d)
    zero_conv = jnp.zeros((1, bp, CONV_W - 1, CONV_DIM), F32)
    zero_ssm = jnp.zeros((1, bp, SSD_STATE, SSD_WIDTH), F32)
    ssm_t = _state_to_lanes(state_ssm.reshape((depth * bs,) + state_ssm.shape[2:]))
    ssm_t = ssm_t.reshape(depth, bs, SSD_STATE, SSD_WIDTH)
    kpe_t = jnp.swapaxes(cache_kpe, 2, 3)
    kv_p = (jnp.zeros((depth, bp * lp, KV_LORA), F32), jnp.zeros((depth, bp * lp, ROPE_DIM), F32))
    kv_s = (jnp.zeros((depth, bs * ls, KV_LORA), F32), jnp.zeros((depth, bs * ls, ROPE_DIM), F32))
    conv_p, conv_s, ssm_p, ssm_s = [], [], [], []
    for i in range(depth):
        w = _layer_weights(i, norm_w, w_in, q_norm_w, w_uq, kv_norm_w, w_uk, w_uv, conv_w, conv_b, dt_bias, a_log,
                           d_skip, ssd_norm_w, w_out)
        yp, kv_p, c_new, h_new = _layer(yp, bp, lp, i, 0, None, zero_conv, zero_ssm, kv_p, w, tab_p, fw)
        conv_p.append(c_new)
        ssm_p.append(_state_from_lanes(h_new))
        ys, kv_s, c_new, h_new = _layer(ys, bs, ls, i, i, (cache_ckv, kpe_t), state_conv, ssm_t, kv_s, w,
                                        tab_s, fw)
        conv_s.append(c_new)
        ssm_s.append(_state_from_lanes(h_new))
    return (yp.reshape(bp, lp, d), ys.reshape(bs, ls, d),
            kv_p[0].reshape(depth, bp, lp, KV_LORA), kv_p[1].reshape(depth, bp, lp, ROPE_DIM),
            jnp.stack(conv_p), jnp.stack(ssm_p),
            kv_s[0].reshape(depth, bs, ls, KV_LORA), kv_s[1].reshape(depth, bs, ls, ROPE_DIM),
            jnp.stack(conv_s), jnp.stack(ssm_s))
```

```python
import functools

import numpy as np
import jax
import jax.numpy as jnp
from jax import lax
from jax.experimental import pallas as pl
from jax.experimental.pallas import tpu as pltpu

F32 = jnp.float32
BF16 = jnp.bfloat16

EPS = 1e-6
CHUNK = 64
MLA_HEADS = 8
Q_LORA = 384
KV_LORA = 256
NOPE_DIM = 128
ROPE_DIM = 64
V_DIM = 128
ROPE_THETA = 10000.0
ATTN_SCALE = (NOPE_DIM + ROPE_DIM) ** -0.5
Q_SCALE = ATTN_SCALE * float(np.log2(np.e))
SSD_HEADS = 16
SSD_HEAD_DIM = 64
SSD_GROUPS = 2
SSD_STATE = 128
CONV_W = 4
SSD_WIDTH = SSD_HEADS * SSD_HEAD_DIM
GROUP_WIDTH = SSD_WIDTH // SSD_GROUPS
HEADS_PER_GROUP = SSD_HEADS // SSD_GROUPS
CONV_DIM = SSD_WIDTH + 2 * SSD_GROUPS * SSD_STATE
MLA_WIDTH = MLA_HEADS * V_DIM
QCAT = KV_LORA + 2 * ROPE_DIM
QK_HEAD = NOPE_DIM + 2 * ROPE_DIM

CUM_PIECES = 3
WIDE_PIECES = 2

LANES = 128
SUBLANES = 8
VMEM_LIMIT = 56 * 1024 * 1024


def _dot(a, b):
    return jnp.dot(a, b, preferred_element_type=F32)


def _rms(x, w):
    return x * lax.rsqrt(jnp.mean(x * x, axis=-1, keepdims=True) + EPS) * w


def _silu(x):
    half = 0.5 * x
    return half + half * jnp.tanh(half)


def _const_spec(shape):
    nd = len(shape)
    return pl.BlockSpec(shape, lambda *_: (0,) * nd)


def _pre_kernel(*refs, seq_rows, tiles_per_seq, n_alias, absorbed):
    refs = refs[n_alias:]
    (x_ref, cst_ref, nw_ref, wq_ref, wkv_ref, wg_ref, wz_ref, wxbc_ref, wsm_ref, qnw_ref, wuqn_ref, wuqp_ref,
     wuqps_ref, kvnw_ref, cw_ref, cb_ref, dtb_ref) = refs[:17]
    n_w, n_out = (1, 2) if absorbed else (2, 3)
    attn_w = refs[17:17 + n_w]
    cos_ref, sin_ref = refs[17 + n_w:19 + n_w]
    attn_out = refs[19 + n_w:19 + n_w + n_out]
    ckv_ref, kpe_ref, g_ref, z_ref, xbc_ref, dt_ref, cnew_ref, buf_sc = refs[19 + n_w + n_out:]
    keep = CONV_W - 1
    base = SUBLANES - keep
    if tiles_per_seq > 1:
        @pl.when(pl.program_id(0) % tiles_per_seq == 0)
        def _():
            buf_sc[0:base, :] = jnp.zeros((base, buf_sc.shape[1]), F32)
            buf_sc[base:SUBLANES, :] = cst_ref[0]

    x = x_ref[...]
    h = _rms(x, nw_ref[...]).astype(BF16)
    xbc = _dot(h, wxbc_ref[...])
    g_ref[...] = _silu(_dot(h, wg_ref[...]))
    z_ref[...] = _silu(_dot(h, wz_ref[...]))

    cos2 = cos_ref[...]
    sin2 = sin_ref[...]
    sm = _dot(h, wsm_ref[...])
    kpe2 = sm[:, 0:LANES] * cos2 + sm[:, LANES:2 * LANES] * sin2
    kpe_ref[...] = kpe2[:, :ROPE_DIM]
    ckv = _rms(_dot(h, wkv_ref[...]), kvnw_ref[...])
    ckv_ref[...] = ckv

    qdt = _dot(h, wq_ref[...])
    dt_in = qdt[:, Q_LORA:Q_LORA + LANES] + dtb_ref[...]
    dt_ref[...] = jnp.maximum(dt_in, 0.0) + jnp.log1p(jnp.exp(-jnp.abs(dt_in)))
    qn = _rms(qdt[:, :Q_LORA], qnw_ref[...]).astype(BF16)
    qnope = _dot(qn, wuqn_ref[...])
    pairs = MLA_HEADS // 2
    cos_all = jnp.concatenate([cos2] * pairs, axis=-1)
    sin_all = jnp.concatenate([sin2] * pairs, axis=-1)
    qpe = _dot(qn, wuqp_ref[...]) * cos_all + _dot(qn, wuqps_ref[...]) * sin_all
    lane = lax.broadcasted_iota(jnp.int32, (x.shape[0], LANES), 1)
    ckv16 = ckv.astype(BF16)
    if absorbed:
        (wuk_ref,) = attn_w
        q_ref, kcat_ref = attn_out
        kcat_ref[...] = jnp.concatenate([ckv16, kpe2.astype(BF16)], axis=-1)
    else:
        wukf_ref, wuvf_ref = attn_w
        q_ref, kh_ref, vh_ref = attn_out
        k_nope = _dot(ckv16, wukf_ref[...])
        v_all = _dot(ckv16, wuvf_ref[...])
    for hd in range(MLA_HEADS):
        cols = slice(hd * NOPE_DIM, (hd + 1) * NOPE_DIM)
        pair = qpe[:, (hd // 2) * LANES:(hd // 2 + 1) * LANES] * Q_SCALE
        own_half = (lane < ROPE_DIM) if hd % 2 == 0 else (lane >= ROPE_DIM)
        q_pe = jnp.where(own_half, pair, 0.0)
        if absorbed:
            q_main = _dot(qnope[:, cols].astype(BF16), wuk_ref[hd]) * Q_SCALE
        else:
            q_main = qnope[:, cols] * Q_SCALE
            kh_ref[hd] = jnp.concatenate([k_nope[:, cols], kpe2], axis=-1).astype(BF16)
            vh_ref[hd] = v_all[:, hd * V_DIM:(hd + 1) * V_DIM].astype(BF16)
        q_ref[hd] = jnp.concatenate([q_main, q_pe], axis=-1).astype(BF16)

    for s in range(x.shape[0] // seq_rows):
        if tiles_per_seq == 1:
            buf_sc[0:base, :] = jnp.zeros((base, buf_sc.shape[1]), F32)
            buf_sc[base:SUBLANES, :] = cst_ref[s]
        buf_sc[SUBLANES:SUBLANES + seq_rows, :] = xbc[s * seq_rows:(s + 1) * seq_rows, :]
        full = buf_sc[...]
        conv = cb_ref[...] + cw_ref[keep:CONV_W, :] * full[SUBLANES:, :]
        for k in range(1, CONV_W):
            shifted = pltpu.roll(full, k, axis=0)[SUBLANES:, :]
            conv = conv + cw_ref[keep - k:CONV_W - k, :] * shifted
        xbc_ref[s * seq_rows:(s + 1) * seq_rows, :] = _silu(conv)
        tail = buf_sc[base + seq_rows:SUBLANES + seq_rows, :]
        buf_sc[base:SUBLANES, :] = tail
        cnew_ref[s] = tail


def _pre_call(x, conv_state, layer, state_layer, stacked, w, cos2, sin2, batch, length, tr, absorbed):
    rows, d = x.shape
    depth = w["depth"]
    seq_rows = min(length, tr)
    tiles_per_seq = length // seq_rows
    nseq = tr // seq_rows
    assert rows == batch * length and length % seq_rows == 0 and tr % seq_rows == 0 and seq_rows >= CONV_W - 1
    row = lambda n: pl.BlockSpec((tr, n), lambda i: (i, 0))
    layer_row = lambda n: pl.BlockSpec((None, tr, n), lambda i: (layer, i, 0))
    heads = lambda n: pl.BlockSpec((MLA_HEADS, tr, n), lambda i: (0, i, 0))
    head_shape = lambda n: jax.ShapeDtypeStruct((MLA_HEADS, rows, n), BF16)
    if absorbed:
        attn_w = (w["w_uk_t"],)
        attn_shape = (head_shape(QCAT), jax.ShapeDtypeStruct((rows, QCAT), BF16))
        attn_specs = (heads(QCAT), row(QCAT))
    else:
        attn_w = (w["w_uk_flat"], w["w_uv_flat"])
        attn_shape = (head_shape(QK_HEAD), head_shape(QK_HEAD), head_shape(V_DIM))
        attn_specs = (heads(QK_HEAD), heads(QK_HEAD), heads(V_DIM))
    weights = (w["norm_w"], w["w_q"], w["w_kv"], w["w_g"], w["w_z"], w["w_xbc"], w["w_sm"],
               w["q_norm_w"], w["w_uq_nope"], w["w_uq_pe"], w["w_uq_pe_sw"], w["kv_norm_w"],
               w["conv_w"], w["conv_b"], w["dt_bias"]) + attn_w
    aliased = tuple(stacked)
    in_specs = ([pl.BlockSpec(memory_space=pl.ANY)] * len(aliased)
                + [row(d), pl.BlockSpec((None, nseq, CONV_W - 1, CONV_DIM),
                                        lambda i: (state_layer, i // tiles_per_seq, 0, 0))]
                + [_const_spec(a.shape) for a in weights] + [row(LANES), row(LANES)])
    out_shape = attn_shape + (
        jax.ShapeDtypeStruct((depth, rows, KV_LORA), F32),
        jax.ShapeDtypeStruct((depth, rows, ROPE_DIM), F32),
        jax.ShapeDtypeStruct((rows, MLA_WIDTH), F32),
        jax.ShapeDtypeStruct((rows, SSD_WIDTH), F32),
        jax.ShapeDtypeStruct((rows, CONV_DIM), F32),
        jax.ShapeDtypeStruct((rows, LANES), F32),
        jax.ShapeDtypeStruct((batch, CONV_W - 1, CONV_DIM), F32),
    )
    out_specs = attn_specs + (
        layer_row(KV_LORA), layer_row(ROPE_DIM), row(MLA_WIDTH), row(SSD_WIDTH), row(CONV_DIM),
        row(LANES), pl.BlockSpec((nseq, CONV_W - 1, CONV_DIM), lambda i: (i // tiles_per_seq, 0, 0)),
    )
    n_attn = len(attn_shape)
    return pl.pallas_call(
        functools.partial(_pre_kernel, seq_rows=seq_rows, tiles_per_seq=tiles_per_seq, n_alias=len(aliased),
                          absorbed=absorbed),
        grid=(rows // tr,), in_specs=in_specs, out_specs=out_specs, out_shape=out_shape,
        scratch_shapes=[pltpu.VMEM((seq_rows + SUBLANES, CONV_DIM), F32)],
        input_output_aliases={k: n_attn + k for k in range(len(aliased))},
        compiler_params=pltpu.CompilerParams(dimension_semantics=("arbitrary",), vmem_limit_bytes=VMEM_LIMIT),
        name="pre",
    )(*aliased, x, conv_state, *weights, cos2, sin2)


def _scores(q, kblk):
    return lax.dot_general(q, kblk, (((1,), (1,)), ((), ())), preferred_element_type=F32)


def _softmax_pv(s, v, m_sc, l_sc, acc_sc, allowed, row0=0):
    rows = slice(row0, row0 + s.shape[0])
    if allowed is not None:
        s = jnp.where(allowed, s, -jnp.inf)
    tk = s.shape[1]
    m_prev = m_sc[rows, :]
    if tk % LANES == 0:
        chunks = [s[:, c * LANES:(c + 1) * LANES] for c in range(tk // LANES)]
        m_new = jnp.maximum(m_prev, jnp.max(functools.reduce(jnp.maximum, chunks), axis=-1, keepdims=True))
        ps = [jnp.exp2(c - m_new) for c in chunks]
        psum = functools.reduce(jnp.add, ps)
        p = jnp.concatenate(ps, axis=-1)
    else:
        m_new = jnp.maximum(m_prev, jnp.max(s, axis=-1, keepdims=True))
        p = jnp.exp2(s - m_new[:, :1])
        psum = jnp.sum(p, axis=-1, keepdims=True) * (1.0 / LANES)
    alpha = jnp.exp2(m_prev - m_new)
    l_sc[rows, :] = alpha * l_sc[rows, :] + psum
    pv = _dot(p.astype(BF16), v)
    acc_sc[rows, :] = jnp.concatenate([alpha] * (acc_sc.shape[1] // LANES), axis=-1) * acc_sc[rows, :] + pv
    m_sc[rows, :] = m_new


def _attn_finish(l_sc, acc_sc, g_ref, wuv_ref, o_ref, tq):
    for hd in range(MLA_HEADS):
        rows = slice(hd * tq, (hd + 1) * tq)
        cols = slice(hd * V_DIM, (hd + 1) * V_DIM)
        inv = 1.0 / jnp.sum(l_sc[rows, :], axis=-1, keepdims=True)
        o = _dot((acc_sc[rows, :] * inv).astype(BF16), wuv_ref[hd])
        o_ref[:, cols] = (o * g_ref[:, cols]).astype(o_ref.dtype)


def _attn_init(m_sc, l_sc, acc_sc):
    m_sc[...] = jnp.full(m_sc.shape, -jnp.inf, F32)
    l_sc[...] = jnp.zeros(l_sc.shape, F32)
    acc_sc[...] = jnp.zeros(acc_sc.shape, F32)


def _attn_scratch(rows, width):
    return [pltpu.VMEM((rows, LANES), F32), pltpu.VMEM((rows, LANES), F32), pltpu.VMEM((rows, width), F32)]


def _attn_prompt_kernel(q_ref, k_ref, v_ref, g_ref, o_ref, m_sc, l_sc, acc_sc, *s_sc, tq, tk):
    i = pl.program_id(1)
    _attn_init(m_sc, l_sc, acc_sc)
    span = 2 * tk
    lower = slice(span, tq)

    def issue_scores(bufs, p, rows):
        for half, s_ref in enumerate(bufs):
            keys = k_ref[pl.ds(pl.multiple_of(p * span + half * tk, tk), tk), :]
            s_ref[rows, :] = _scores(q_ref[rows, :], keys)

    def softmax(bufs, p, rows, own_pair=None):
        allowed = None
        if own_pair is not None:
            qchunk = (rows.start + lax.broadcasted_iota(jnp.int32, (rows.stop - rows.start, 1), 0)) // CHUNK
            kchunk = (own_pair * span + lax.broadcasted_iota(jnp.int32, (1, span), 1)) // CHUNK
            allowed = kchunk <= qchunk
        s = jnp.concatenate([s_ref[rows, :] for s_ref in bufs], axis=-1)
        _softmax_pv(s, v_ref[pl.ds(pl.multiple_of(p * span, span), span), :], m_sc, l_sc, acc_sc, allowed,
                    rows.start)

    sa, sb = s_sc[:2], s_sc[2:]
    every = slice(0, tq)
    issue_scores(sa, 0, every)

    def earlier_tile(t, carry):
        issue_scores(sb, 2 * t + 1, every)
        softmax(sa, 2 * t, every)
        issue_scores(sa, 2 * t + 2, every)
        softmax(sb, 2 * t + 1, every)
        return carry

    lax.fori_loop(0, i, earlier_tile, 0)
    issue_scores(sb, 2 * i + 1, lower)
    softmax(sa, 2 * i, every, own_pair=0)
    softmax(sb, 2 * i + 1, lower, own_pair=1)
    inv = 1.0 / jnp.sum(l_sc[...], axis=-1, keepdims=True)
    o_ref[...] = (acc_sc[...] * inv * g_ref[...]).astype(o_ref.dtype)


def _attn_prompt_call(qh, kh, vh, g, tq, tk):
    heads, rows, _ = qh.shape
    assert tk % CHUNK == 0 and rows % tq == 0 and tq == 4 * tk
    per_head = lambda n, m: pl.BlockSpec((None, n, m), lambda hd, i: (hd, 0, 0), pipeline_mode=pl.Buffered(1))
    return pl.pallas_call(
        functools.partial(_attn_prompt_kernel, tq=tq, tk=tk),
        grid=(heads, rows // tq),
        in_specs=[pl.BlockSpec((None, tq, QK_HEAD), lambda hd, i: (hd, i, 0)),
                  per_head(rows, QK_HEAD), per_head(rows, V_DIM),
                  pl.BlockSpec((tq, V_DIM), lambda hd, i: (i, hd))],
        out_specs=pl.BlockSpec((tq, V_DIM), lambda hd, i: (i, hd)),
        out_shape=jax.ShapeDtypeStruct((rows, heads * V_DIM), BF16),
        scratch_shapes=_attn_scratch(tq, V_DIM) + [pltpu.VMEM((tq, tk), F32)] * 4,
        compiler_params=pltpu.CompilerParams(dimension_semantics=("arbitrary", "arbitrary"),
                                             vmem_limit_bytes=VMEM_LIMIT),
        name="attn_prompt",
    )(qh, kh, vh, g)


def _attn_sample_kernel(q_ref, pckv_ref, pkpet_ref, knew_ref, g_ref, wuv_ref, o_ref, m_sc, l_sc, acc_sc, *, tq, tk):
    q = q_ref[...].reshape(MLA_HEADS * tq, QCAT)
    _attn_init(m_sc, l_sc, acc_sc)
    past = pckv_ref.shape[0]
    q_lat = q[:, :KV_LORA]
    pair = q[:, KV_LORA:]
    q_pe = pair[:, :ROPE_DIM] + pair[:, ROPE_DIM:]

    def body(j, carry):
        r0 = pl.multiple_of(j * tk, tk)
        ckv = pckv_ref[pl.ds(r0, tk), :].astype(BF16)
        s = _scores(q_lat, ckv) + _dot(q_pe, pkpet_ref[:, pl.ds(r0, tk)].astype(BF16))
        _softmax_pv(s, ckv, m_sc, l_sc, acc_sc, None)
        return carry

    lax.fori_loop(0, past // tk, body, 0)
    knew = knew_ref[...]
    _softmax_pv(_scores(q, knew), knew[:, :KV_LORA], m_sc, l_sc, acc_sc, None)
    _attn_finish(l_sc, acc_sc, g_ref, wuv_ref, o_ref, tq)


def _attn_sample_call(qcat, past_ckv, past_kpe_t, layer, kcat, g, wuv_t, batch, length, tk):
    past = past_ckv.shape[2]
    assert past % CHUNK == 0 and length <= CHUNK and past % tk == 0
    rows = batch * length
    r8 = MLA_HEADS * length
    return pl.pallas_call(
        functools.partial(_attn_sample_kernel, tq=length, tk=tk),
        grid=(batch,),
        in_specs=[pl.BlockSpec((MLA_HEADS, length, QCAT), lambda b: (0, b, 0)),
                  pl.BlockSpec((None, None, past, KV_LORA), lambda b: (layer, b, 0, 0)),
                  pl.BlockSpec((None, None, ROPE_DIM, past), lambda b: (layer, b, 0, 0)),
                  pl.BlockSpec((length, QCAT), lambda b: (b, 0)),
                  pl.BlockSpec((length, MLA_WIDTH), lambda b: (b, 0)),
                  _const_spec(wuv_t.shape)],
        out_specs=pl.BlockSpec((length, MLA_WIDTH), lambda b: (b, 0)),
        out_shape=jax.ShapeDtypeStruct((rows, MLA_WIDTH), BF16),
        scratch_shapes=_attn_scratch(r8, KV_LORA),
        compiler_params=pltpu.CompilerParams(dimension_semantics=("arbitrary",), vmem_limit_bytes=VMEM_LIMIT),
        name="attn_sample",
    )(qcat, past_ckv, past_kpe_t, kcat, g, wuv_t)


def _ssd_constants(cl):
    hb = LANES // cl
    ltri = np.tril(np.ones((cl, cl), np.float32))
    exp_s = np.zeros((LANES, SSD_HEADS * cl), np.float32)
    exp_p = np.zeros((LANES, SSD_WIDTH), np.float32)
    for hd in range(SSD_HEADS):
        exp_s[hd, hd * cl:(hd + 1) * cl] = 1.0
        exp_p[hd, hd * SSD_HEAD_DIM:(hd + 1) * SSD_HEAD_DIM] = 1.0
    t = np.arange(cl)[:, None]
    s = np.tile(np.arange(cl), SSD_HEADS)[None, :]
    eye = (t == s).astype(np.float32)
    tril = (t >= s).astype(np.float32)
    hrow = np.repeat(np.arange(hb), cl)[:, None]
    hcol = np.repeat(np.arange(hb), SSD_HEAD_DIM)[None, :]
    bdiag = (hrow == hcol).astype(np.float32)
    bf = lambda a: jnp.asarray(a, BF16)
    return (bf(np.tile(ltri, (1, CUM_PIECES))), bf(np.tile(exp_s, (CUM_PIECES, 1))),
            bf(np.tile(exp_p, (WIDE_PIECES, 1))), jnp.asarray(eye), jnp.asarray(tril), bf(bdiag))


def _split(x, n):
    pieces = []
    for _ in range(n):
        p = x.astype(BF16)
        pieces.append(p)
        x = x - p.astype(F32)
    return pieces


def _expand(x, sel, n):
    return _dot(jnp.concatenate(_split(x, n), axis=-1), sel)


def _ssd_kernel(act_ref, dt_ref, z_ref, h0_ref, alog_ref, dsk_ref, nw_ref,
                ltri_ref, exps_ref, expp_ref, eye_ref, tril_ref, bdiag_ref,
                y_ref, hout_ref, h_sc, wide_sc, e_sc, *, cl, cps):
    step = pl.program_id(1)
    last = pl.num_programs(1) - 1
    rows = cl * cps

    @pl.when(step == 0)
    def _():
        h_sc[...] = h0_ref[...]

    dt = dt_ref[...]
    a_pieces = _split(dt * -jnp.exp(alog_ref[...]), CUM_PIECES)
    cums, totals = [], []
    for c in range(cps):
        stacked = jnp.concatenate([p[c * cl:(c + 1) * cl, :] for p in a_pieces], axis=0)
        cum_c = _dot(ltri_ref[...], stacked)
        cums.append(cum_c)
        totals.append(jnp.broadcast_to(cum_c[cl - 1:cl, :], cum_c.shape))
    cum = jnp.concatenate(cums, axis=0)
    total = jnp.concatenate(totals, axis=0)
    wide_sc[...] = _expand(jnp.concatenate([dt, jnp.exp(total - cum), jnp.exp(cum)], axis=0), expp_ref[...],
                           WIDE_PIECES)
    e_sc[...] = _expand(cum, exps_ref[...], CUM_PIECES)

    hb = LANES // cl
    gs = GROUP_WIDTH
    gm = HEADS_PER_GROUP * cl

    def chunk(c, carry):
        r0 = pl.multiple_of(c * cl, cl)
        xs = act_ref[pl.ds(r0, cl), 0:SSD_WIDTH]
        bm = act_ref[pl.ds(r0, cl), SSD_WIDTH:SSD_WIDTH + SSD_GROUPS * SSD_STATE]
        cm = act_ref[pl.ds(r0, cl), SSD_WIDTH + SSD_GROUPS * SSD_STATE:CONV_DIM]
        dt_w = wide_sc[pl.ds(r0, cl), :]
        dstate_w = wide_sc[pl.ds(pl.multiple_of(rows + r0, cl), cl), :]
        ecum_w = wide_sc[pl.ds(pl.multiple_of(2 * rows + r0, cl), cl), :]
        e = e_sc[pl.ds(r0, cl), :]
        f_row = jnp.sum(jnp.where(eye_ref[...] > 0.0, e, 0.0), axis=0, keepdims=True)
        decay = jnp.where(tril_ref[...] > 0.0, jnp.exp(e - f_row), 0.0)
        xd = xs * dt_w
        w_state = (xd * dstate_w).astype(BF16)
        xd16 = xd.astype(BF16)
        h_prev = h_sc[...]
        h16 = h_prev.astype(BF16)
        bdiag = bdiag_ref[...]
        y_parts, st_parts = [], []
        for g in range(SSD_GROUPS):
            bg = bm[:, g * SSD_STATE:(g + 1) * SSD_STATE].astype(BF16)
            cg = cm[:, g * SSD_STATE:(g + 1) * SSD_STATE].astype(BF16)
            cb = lax.dot_general(cg, jnp.concatenate([bg] * hb, axis=0), (((1,), (1,)), ((), ())),
                                 preferred_element_type=F32)
            y_off = _dot(cg, h16[:, g * gs:(g + 1) * gs]) * ecum_w[:, g * gs:(g + 1) * gs]
            diag = []
            for j in range(gm // LANES):
                m = (cb * decay[:, g * gm + j * LANES:g * gm + (j + 1) * LANES]).astype(BF16)
                c0 = g * gs + j * hb * SSD_HEAD_DIM
                xj = xd16[:, c0:c0 + hb * SSD_HEAD_DIM]
                diag.append(_dot(m, jnp.concatenate([xj] * hb, axis=0) * bdiag))
            y_parts.append(y_off + jnp.concatenate(diag, axis=-1))
            st_parts.append(lax.dot_general(bg, w_state[:, g * gs:(g + 1) * gs], (((0,), (0,)), ((), ())),
                                            preferred_element_type=F32))
        h_sc[...] = h_prev * ecum_w[cl - 1:cl, :] + jnp.concatenate(st_parts, axis=-1)
        y = jnp.concatenate(y_parts, axis=-1) + xs * dsk_ref[...]
        y = y * z_ref[pl.ds(r0, cl), :]
        normed = [_rms(y[:, g * gs:(g + 1) * gs], nw_ref[:, g * gs:(g + 1) * gs]) for g in range(SSD_GROUPS)]
        y_ref[pl.ds(r0, cl), :] = jnp.concatenate(normed, axis=-1).astype(y_ref.dtype)
        return carry

    lax.fori_loop(0, cps, chunk, 0, unroll=True)

    @pl.when(step == last)
    def _():
        hout_ref[...] = h_sc[...]


def _ssd_call(act, dt, z, h0_t, state_layer, w, batch, length, cps):
    cl = min(CHUNK, length)
    rows = cl * cps
    assert length % rows == 0 and LANES % cl == 0
    consts = _ssd_constants(cl)
    seq = lambda n: pl.BlockSpec((None, rows, n), lambda b, s: (b, s, 0))
    params = (w["a_log"], w["d_skip"], w["ssd_norm_w"])
    in_specs = ([seq(CONV_DIM), seq(LANES), seq(SSD_WIDTH),
                 pl.BlockSpec((None, None, SSD_STATE, SSD_WIDTH), lambda b, s: (state_layer, b, 0, 0))]
                + [pl.BlockSpec(a.shape, lambda b, s: (0, 0)) for a in params + consts])
    out_shape = (jax.ShapeDtypeStruct((batch, length, SSD_WIDTH), BF16),
                 jax.ShapeDtypeStruct((batch, SSD_STATE, SSD_WIDTH), F32))
    out_specs = (seq(SSD_WIDTH), pl.BlockSpec((None, SSD_STATE, SSD_WIDTH), lambda b, s: (b, 0, 0)))
    return pl.pallas_call(
        functools.partial(_ssd_kernel, cl=cl, cps=cps),
        grid=(batch, length // rows), in_specs=in_specs, out_specs=out_specs, out_shape=out_shape,
        scratch_shapes=[pltpu.VMEM((SSD_STATE, SSD_WIDTH), F32), pltpu.VMEM((3 * rows, SSD_WIDTH), F32),
                        pltpu.VMEM((rows, SSD_HEADS * cl), F32)],
        compiler_params=pltpu.CompilerParams(dimension_semantics=("arbitrary", "arbitrary"),
                                             vmem_limit_bytes=VMEM_LIMIT),
        name="ssd",
    )(act.reshape(batch, length, CONV_DIM), dt.reshape(batch, length, LANES),
      z.reshape(batch, length, SSD_WIDTH), h0_t, *params, *consts)


def _post_kernel(mla_ref, ssd_ref, x_ref, wa_ref, wb_ref, fw_ref, o_ref, *, final):
    y = x_ref[...] + _dot(mla_ref[...], wa_ref[...]) + _dot(ssd_ref[...], wb_ref[...])
    o_ref[...] = _rms(y, fw_ref[...]) if final else y


def _post_call(mla, ssd, x, w, final_norm_w, final, tr):
    rows, d = x.shape
    row = lambda n: pl.BlockSpec((tr, n), lambda i: (i, 0))
    return pl.pallas_call(
        functools.partial(_post_kernel, final=final),
        grid=(rows // tr,),
        in_specs=[row(MLA_WIDTH), row(SSD_WIDTH), row(d), _const_spec(w["w_out_mla"].shape),
                  _const_spec(w["w_out_ssd"].shape), _const_spec(final_norm_w.shape)],
        out_specs=row(d), out_shape=jax.ShapeDtypeStruct((rows, d), F32),
        compiler_params=pltpu.CompilerParams(dimension_semantics=("arbitrary",), vmem_limit_bytes=VMEM_LIMIT),
        name="post",
    )(mla, ssd, x, w["w_out_mla"], w["w_out_ssd"], final_norm_w)


def _swap_halves(wcols):
    k, n = wcols.shape
    blocks = wcols.reshape(k, n // ROPE_DIM, 2, ROPE_DIM // 2)
    return blocks[:, :, ::-1, :].reshape(k, n)


def _layer_weights(i, norm_w, w_in, q_norm_w, w_uq, kv_norm_w, w_uk, w_uv, conv_w, conv_b, dt_bias, a_log, d_skip,
                   ssd_norm_w, w_out):
    d = w_in.shape[1]
    offs = np.cumsum((0, Q_LORA, KV_LORA, ROPE_DIM, MLA_WIDTH, SSD_WIDTH, CONV_DIM, SSD_HEADS))
    col = lambda j: w_in[i][:, offs[j]:offs[j + 1]]
    w_kpe = col(2)
    w_kpe_sw = _swap_halves(w_kpe)
    w_sm = jnp.concatenate([w_kpe, w_kpe, w_kpe_sw, w_kpe_sw], axis=1)
    w_qdt = jnp.concatenate([col(0), col(6), jnp.zeros((d, LANES - SSD_HEADS), F32)], axis=1)
    uq = w_uq[i].reshape(Q_LORA, MLA_HEADS, NOPE_DIM + ROPE_DIM)
    uq_pe = uq[:, :, NOPE_DIM:].reshape(Q_LORA, MLA_HEADS * ROPE_DIM)
    pad_heads = lambda v: jnp.concatenate([v, jnp.zeros((LANES - SSD_HEADS,), F32)])[None, :]
    return {
        "depth": w_in.shape[0],
        "norm_w": norm_w[i][None, :],
        "w_q": w_qdt.astype(BF16), "w_kv": col(1).astype(BF16), "w_g": col(3).astype(BF16),
        "w_z": col(4).astype(BF16), "w_xbc": col(5).astype(BF16), "w_sm": w_sm.astype(BF16),
        "q_norm_w": q_norm_w[i][None, :],
        "w_uq_nope": uq[:, :, :NOPE_DIM].reshape(Q_LORA, MLA_HEADS * NOPE_DIM).astype(BF16),
        "w_uq_pe": uq_pe.astype(BF16), "w_uq_pe_sw": _swap_halves(uq_pe).astype(BF16),
        "kv_norm_w": kv_norm_w[i][None, :],
        "w_uk_t": jnp.transpose(w_uk[i], (1, 2, 0)).astype(BF16),
        "w_uv_t": jnp.transpose(w_uv[i], (1, 0, 2)).astype(BF16),
        "w_uk_flat": w_uk[i].reshape(KV_LORA, MLA_HEADS * NOPE_DIM).astype(BF16),
        "w_uv_flat": w_uv[i].reshape(KV_LORA, MLA_HEADS * V_DIM).astype(BF16),
        "conv_w": conv_w[i], "conv_b": conv_b[i][None, :],
        "dt_bias": pad_heads(dt_bias[i]), "a_log": pad_heads(a_log[i]),
        "d_skip": jnp.repeat(d_skip[i], SSD_HEAD_DIM)[None, :],
        "ssd_norm_w": ssd_norm_w[i][None, :],
        "w_out_mla": w_out[i][:MLA_WIDTH].astype(BF16), "w_out_ssd": w_out[i][MLA_WIDTH:].astype(BF16),
    }


def _rope_tables(past, length, batch):
    half = ROPE_DIM // 2
    inv = 1.0 / (ROPE_THETA ** (np.arange(half, dtype=np.float64) / half))
    ang = (past + np.arange(length, dtype=np.float64))[:, None] * inv[None, :]
    cos, sin = np.cos(ang), np.sin(ang)
    reps = LANES // ROPE_DIM
    cos2 = np.tile(np.concatenate([cos, cos], axis=-1), (batch, reps))
    sin2 = np.tile(np.concatenate([-sin, sin], axis=-1), (batch, reps))
    return jnp.asarray(cos2, F32), jnp.asarray(sin2, F32)


def _state_to_lanes(h):
    b = h.shape[0]
    return jnp.transpose(h, (0, 3, 1, 2)).reshape(b, SSD_STATE, SSD_WIDTH)


def _state_from_lanes(ht):
    b = ht.shape[0]
    return jnp.transpose(ht.reshape(b, SSD_STATE, SSD_HEADS, SSD_HEAD_DIM), (0, 2, 3, 1))


def _pick(n, prefs):
    for p in prefs:
        if n % p == 0:
            return p
    return n


def _layer(x, batch, length, layer, state_layer, caches, conv_state, ssm_state_t, stacked, w, tables, final_norm_w):
    rows = batch * length
    tr = _pick(rows, (512, 256, 128))
    absorbed = caches is not None
    *attn_in, ckv, kpe, g, z, act, dt, conv_new = _pre_call(x, conv_state, layer, state_layer, stacked, w, tables[0],
                                                            tables[1], batch, length, tr, absorbed)
    if absorbed:
        qcat, kcat = attn_in
        mla = _attn_sample_call(qcat, caches[0], caches[1], layer, kcat, g, w["w_uv_t"], batch, length,
                                _pick(caches[0].shape[2], (2048, 1024, 512, 256, 128, 64)))
    else:
        assert batch == 1
        tq = _pick(length, (2048, 1024, 512))
        mla = _attn_prompt_call(*attn_in, g, tq, min(tq // 4, 512))
    cl = min(CHUNK, length)
    ssd, h_t = _ssd_call(act, dt, z, ssm_state_t, state_layer, w, batch, length, _pick(length // cl, (8, 4, 2, 1)))
    y = _post_call(mla, ssd.reshape(rows, SSD_WIDTH), x, w, final_norm_w, layer == w["depth"] - 1,
                   _pick(rows, (1024, 512, 256, 128)))
    return y, (ckv, kpe), conv_new, h_t


def kernel(x_prompt, x_sample, cache_ckv, cache_kpe, state_conv, state_ssm, norm_w, w_in, q_norm_w, w_uq,
           kv_norm_w, w_uk, w_uv, conv_w, conv_b, dt_bias, a_log, d_skip, ssd_norm_w, w_out, final_norm_w):
    depth = w_in.shape[0]
    bp, lp, d = x_prompt.shape
    bs, ls, _ = x_sample.shape
    past = cache_ckv.shape[2]
    tab_p = _rope_tables(0, lp, bp)
    tab_s = _rope_tables(past, ls, bs)
    fw = final_norm_w[None, :]
    yp, ys = x_prompt.reshape(bp * lp, d), x_sample.reshape(bs * ls, d)
    zero_conv = jnp.zeros((1, bp, CONV_W - 1, CONV_DIM), F32)
    zero_ssm = jnp.zeros((1, bp, SSD_STATE, SSD_WIDTH), F32)
    ssm_t = _state_to_lanes(state_ssm.reshape((depth * bs,) + state_ssm.shape[2:]))
    ssm_t = ssm_t.reshape(depth, bs, SSD_STATE, SSD_WIDTH)
    kpe_t = jnp.swapaxes(cache_kpe, 2, 3)
    kv_p = (jnp.zeros((depth, bp * lp, KV_LORA), F32), jnp.zeros((depth, bp * lp, ROPE_DIM), F32))
    kv_s = (jnp.zeros((depth, bs * ls, KV_LORA), F32), jnp.zeros((depth, bs * ls, ROPE_DIM), F32))
    conv_p, conv_s, ssm_p, ssm_s = [], [], [], []
    for i in range(depth):
        w = _layer_weights(i, norm_w, w_in, q_norm_w, w_uq, kv_norm_w, w_uk, w_uv, conv_w, conv_b, dt_bias, a_log,
                           d_skip, ssd_norm_w, w_out)
        yp, kv_p, c_new, h_new = _layer(yp, bp, lp, i, 0, None, zero_conv, zero_ssm, kv_p, w, tab_p, fw)
        conv_p.append(c_new)
        ssm_p.append(_state_from_lanes(h_new))
        ys, kv_s, c_new, h_new = _layer(ys, bs, ls, i, i, (cache_ckv, kpe_t), state_conv, ssm_t, kv_s, w,
                                        tab_s, fw)
        conv_s.append(c_new)
        ssm_s.append(_state_from_lanes(h_new))
    return (yp.reshape(bp, lp, d), ys.reshape(bs, ls, d),
            kv_p[0].reshape(depth, bp, lp, KV_LORA), kv_p[1].reshape(depth, bp, lp, ROPE_DIM),
            jnp.stack(conv_p), jnp.stack(ssm_p),
            kv_s[0].reshape(depth, bs, ls, KV_LORA), kv_s[1].reshape(depth, bs, ls, ROPE_DIM),
            jnp.stack(conv_s), jnp.stack(ssm_s))
```

```python
import functools

import numpy as np
import jax
import jax.numpy as jnp
from jax import lax
from jax.experimental import pallas as pl
from jax.experimental.pallas import tpu as pltpu

F32 = jnp.float32
BF16 = jnp.bfloat16

EPS = 1e-6
CHUNK = 64
MLA_HEADS = 8
Q_LORA = 384
KV_LORA = 256
NOPE_DIM = 128
ROPE_DIM = 64
V_DIM = 128
ROPE_THETA = 10000.0
ATTN_SCALE = (NOPE_DIM + ROPE_DIM) ** -0.5
Q_SCALE = ATTN_SCALE * float(np.log2(np.e))
SSD_HEADS = 16
SSD_HEAD_DIM = 64
SSD_GROUPS = 2
SSD_STATE = 128
CONV_W = 4
SSD_WIDTH = SSD_HEADS * SSD_HEAD_DIM
GROUP_WIDTH = SSD_WIDTH // SSD_GROUPS
HEADS_PER_GROUP = SSD_HEADS // SSD_GROUPS
CONV_DIM = SSD_WIDTH + 2 * SSD_GROUPS * SSD_STATE
MLA_WIDTH = MLA_HEADS * V_DIM
QCAT = KV_LORA + 2 * ROPE_DIM
QK_HEAD = NOPE_DIM + 2 * ROPE_DIM

CUM_PIECES = 3
WIDE_PIECES = 2

LANES = 128
SUBLANES = 8
VMEM_LIMIT = 60 * 1024 * 1024


def _dot(a, b):
    return jnp.dot(a, b, preferred_element_type=F32)


def _rms(x, w):
    return x * lax.rsqrt(jnp.mean(x * x, axis=-1, keepdims=True) + EPS) * w


def _silu(x):
    half = 0.5 * x
    return half + half * jnp.tanh(half)


def _const_spec(shape):
    nd = len(shape)
    return pl.BlockSpec(shape, lambda *_: (0,) * nd)


def _pre_kernel(*refs, seq_rows, tiles_per_seq, n_alias, absorbed):
    refs = refs[n_alias:]
    (x_ref, cst_ref, nw_ref, wq_ref, wkv_ref, wg_ref, wz_ref, wxbc_ref, wsm_ref, qnw_ref, wuqn_ref, wuqp_ref,
     wuqps_ref, kvnw_ref, cw_ref, cb_ref, dtb_ref) = refs[:17]
    n_w, n_out = (1, 2) if absorbed else (2, 3)
    attn_w = refs[17:17 + n_w]
    cos_ref, sin_ref = refs[17 + n_w:19 + n_w]
    attn_out = refs[19 + n_w:19 + n_w + n_out]
    ckv_ref, kpe_ref, g_ref, z_ref, xbc_ref, dt_ref, cnew_ref, buf_sc = refs[19 + n_w + n_out:]
    keep = CONV_W - 1
    base = SUBLANES - keep
    if tiles_per_seq > 1:
        @pl.when(pl.program_id(0) % tiles_per_seq == 0)
        def _():
            buf_sc[0:base, :] = jnp.zeros((base, buf_sc.shape[1]), F32)
            buf_sc[base:SUBLANES, :] = cst_ref[0]

    x = x_ref[...]
    h = _rms(x, nw_ref[...]).astype(BF16)
    xbc = _dot(h, wxbc_ref[...])
    g_ref[...] = _silu(_dot(h, wg_ref[...]))
    z_ref[...] = _silu(_dot(h, wz_ref[...]))

    cos2 = cos_ref[...]
    sin2 = sin_ref[...]
    sm = _dot(h, wsm_ref[...])
    kpe2 = sm[:, 0:LANES] * cos2 + sm[:, LANES:2 * LANES] * sin2
    kpe_ref[...] = kpe2[:, :ROPE_DIM]
    ckv = _rms(_dot(h, wkv_ref[...]), kvnw_ref[...])
    ckv_ref[...] = ckv

    qdt = _dot(h, wq_ref[...])
    dt_in = qdt[:, Q_LORA:Q_LORA + LANES] + dtb_ref[...]
    dt_ref[...] = jnp.maximum(dt_in, 0.0) + jnp.log1p(jnp.exp(-jnp.abs(dt_in)))
    qn = _rms(qdt[:, :Q_LORA], qnw_ref[...]).astype(BF16)
    qnope = _dot(qn, wuqn_ref[...])
    pairs = MLA_HEADS // 2
    cos_all = jnp.concatenate([cos2] * pairs, axis=-1)
    sin_all = jnp.concatenate([sin2] * pairs, axis=-1)
    qpe = _dot(qn, wuqp_ref[...]) * cos_all + _dot(qn, wuqps_ref[...]) * sin_all
    lane = lax.broadcasted_iota(jnp.int32, (x.shape[0], LANES), 1)
    ckv16 = ckv.astype(BF16)
    if absorbed:
        (wuk_ref,) = attn_w
        q_ref, kcat_ref = attn_out
        kcat_ref[...] = jnp.concatenate([ckv16, kpe2.astype(BF16)], axis=-1)
    else:
        wukf_ref, wuvf_ref = attn_w
        q_ref, kh_ref, vh_ref = attn_out
        k_nope = _dot(ckv16, wukf_ref[...])
        v_all = _dot(ckv16, wuvf_ref[...])
    for hd in range(MLA_HEADS):
        cols = slice(hd * NOPE_DIM, (hd + 1) * NOPE_DIM)
        pair = qpe[:, (hd // 2) * LANES:(hd // 2 + 1) * LANES] * Q_SCALE
        own_half = (lane < ROPE_DIM) if hd % 2 == 0 else (lane >= ROPE_DIM)
        q_pe = jnp.where(own_half, pair, 0.0)
        if absorbed:
            q_main = _dot(qnope[:, cols].astype(BF16), wuk_ref[hd]) * Q_SCALE
        else:
            q_main = qnope[:, cols] * Q_SCALE
            kh_ref[hd] = jnp.concatenate([k_nope[:, cols], kpe2], axis=-1).astype(BF16)
            vh_ref[hd] = v_all[:, hd * V_DIM:(hd + 1) * V_DIM].astype(BF16)
        q_ref[hd] = jnp.concatenate([q_main, q_pe], axis=-1).astype(BF16)

    for s in range(x.shape[0] // seq_rows):
        if tiles_per_seq == 1:
            buf_sc[0:base, :] = jnp.zeros((base, buf_sc.shape[1]), F32)
            buf_sc[base:SUBLANES, :] = cst_ref[s]
        buf_sc[SUBLANES:SUBLANES + seq_rows, :] = xbc[s * seq_rows:(s + 1) * seq_rows, :]
        full = buf_sc[...]
        conv = cb_ref[...] + cw_ref[keep:CONV_W, :] * full[SUBLANES:, :]
        for k in range(1, CONV_W):
            shifted = pltpu.roll(full, k, axis=0)[SUBLANES:, :]
            conv = conv + cw_ref[keep - k:CONV_W - k, :] * shifted
        xbc_ref[s * seq_rows:(s + 1) * seq_rows, :] = _silu(conv)
        tail = buf_sc[base + seq_rows:SUBLANES + seq_rows, :]
        buf_sc[base:SUBLANES, :] = tail
        cnew_ref[s] = tail


def _pre_call(x, conv_state, layer, state_layer, stacked, w, cos2, sin2, batch, length, tr, absorbed):
    rows, d = x.shape
    depth = w["depth"]
    seq_rows = min(length, tr)
    tiles_per_seq = length // seq_rows
    nseq = tr // seq_rows
    assert rows == batch * length and length % seq_rows == 0 and tr % seq_rows == 0 and seq_rows >= CONV_W - 1
    row = lambda n: pl.BlockSpec((tr, n), lambda i: (i, 0))
    layer_row = lambda n: pl.BlockSpec((None, tr, n), lambda i: (layer, i, 0))
    heads = lambda n: pl.BlockSpec((MLA_HEADS, tr, n), lambda i: (0, i, 0))
    head_shape = lambda n: jax.ShapeDtypeStruct((MLA_HEADS, rows, n), BF16)
    if absorbed:
        attn_w = (w["w_uk_t"],)
        attn_shape = (head_shape(QCAT), jax.ShapeDtypeStruct((rows, QCAT), BF16))
        attn_specs = (heads(QCAT), row(QCAT))
    else:
        attn_w = (w["w_uk_flat"], w["w_uv_flat"])
        attn_shape = (head_shape(QK_HEAD), head_shape(QK_HEAD), head_shape(V_DIM))
        attn_specs = (heads(QK_HEAD), heads(QK_HEAD), heads(V_DIM))
    weights = (w["norm_w"], w["w_q"], w["w_kv"], w["w_g"], w["w_z"], w["w_xbc"], w["w_sm"],
               w["q_norm_w"], w["w_uq_nope"], w["w_uq_pe"], w["w_uq_pe_sw"], w["kv_norm_w"],
               w["conv_w"], w["conv_b"], w["dt_bias"]) + attn_w
    aliased = tuple(stacked)
    in_specs = ([pl.BlockSpec(memory_space=pl.ANY)] * len(aliased)
                + [row(d), pl.BlockSpec((None, nseq, CONV_W - 1, CONV_DIM),
                                        lambda i: (state_layer, i // tiles_per_seq, 0, 0))]
                + [_const_spec(a.shape) for a in weights] + [row(LANES), row(LANES)])
    out_shape = attn_shape + (
        jax.ShapeDtypeStruct((depth, rows, KV_LORA), F32),
        jax.ShapeDtypeStruct((depth, rows, ROPE_DIM), F32),
        jax.ShapeDtypeStruct((rows, MLA_WIDTH), F32),
        jax.ShapeDtypeStruct((rows, SSD_WIDTH), F32),
        jax.ShapeDtypeStruct((rows, CONV_DIM), F32),
        jax.ShapeDtypeStruct((rows, LANES), F32),
        jax.ShapeDtypeStruct((batch, CONV_W - 1, CONV_DIM), F32),
    )
    out_specs = attn_specs + (
        layer_row(KV_LORA), layer_row(ROPE_DIM), row(MLA_WIDTH), row(SSD_WIDTH), row(CONV_DIM),
        row(LANES), pl.BlockSpec((nseq, CONV_W - 1, CONV_DIM), lambda i: (i // tiles_per_seq, 0, 0)),
    )
    n_attn = len(attn_shape)
    return pl.pallas_call(
        functools.partial(_pre_kernel, seq_rows=seq_rows, tiles_per_seq=tiles_per_seq, n_alias=len(aliased),
                          absorbed=absorbed),
        grid=(rows // tr,), in_specs=in_specs, out_specs=out_specs, out_shape=out_shape,
        scratch_shapes=[pltpu.VMEM((seq_rows + SUBLANES, CONV_DIM), F32)],
        input_output_aliases={k: n_attn + k for k in range(len(aliased))},
        compiler_params=pltpu.CompilerParams(dimension_semantics=("arbitrary",), vmem_limit_bytes=VMEM_LIMIT),
        name="pre",
    )(*aliased, x, conv_state, *weights, cos2, sin2)


def _scores(q, kblk):
    return lax.dot_general(q, kblk, (((1,), (1,)), ((), ())), preferred_element_type=F32)


def _softmax_pv(s, v, m_sc, l_sc, acc_sc, allowed, row0=0):
    rows = slice(row0, row0 + s.shape[0])
    if allowed is not None:
        s = jnp.where(allowed, s, -jnp.inf)
    tk = s.shape[1]
    m_prev = m_sc[rows, :]
    if tk % LANES == 0:
        chunks = [s[:, c * LANES:(c + 1) * LANES] for c in range(tk // LANES)]
        m_new = jnp.maximum(m_prev, jnp.max(functools.reduce(jnp.maximum, chunks), axis=-1, keepdims=True))
        ps = [jnp.exp2(c - m_new) for c in chunks]
        psum = functools.reduce(jnp.add, ps)
        p = jnp.concatenate(ps, axis=-1)
    else:
        m_new = jnp.maximum(m_prev, jnp.max(s, axis=-1, keepdims=True))
        p = jnp.exp2(s - m_new[:, :1])
        psum = jnp.sum(p, axis=-1, keepdims=True) * (1.0 / LANES)
    alpha = jnp.exp2(m_prev - m_new)
    l_sc[rows, :] = alpha * l_sc[rows, :] + psum
    pv = _dot(p.astype(BF16), v)
    acc_sc[rows, :] = jnp.concatenate([alpha] * (acc_sc.shape[1] // LANES), axis=-1) * acc_sc[rows, :] + pv
    m_sc[rows, :] = m_new


def _attn_finish(l_sc, acc_sc, g_ref, wuv_ref, o_ref, tq):
    for hd in range(MLA_HEADS):
        rows = slice(hd * tq, (hd + 1) * tq)
        cols = slice(hd * V_DIM, (hd + 1) * V_DIM)
        inv = 1.0 / jnp.sum(l_sc[rows, :], axis=-1, keepdims=True)
        o = _dot((acc_sc[rows, :] * inv).astype(BF16), wuv_ref[hd])
        o_ref[:, cols] = (o * g_ref[:, cols]).astype(o_ref.dtype)


def _attn_init(m_sc, l_sc, acc_sc):
    m_sc[...] = jnp.full(m_sc.shape, -jnp.inf, F32)
    l_sc[...] = jnp.zeros(l_sc.shape, F32)
    acc_sc[...] = jnp.zeros(acc_sc.shape, F32)


def _attn_scratch(rows, width):
    return [pltpu.VMEM((rows, LANES), F32), pltpu.VMEM((rows, LANES), F32), pltpu.VMEM((rows, width), F32)]


def _attn_prompt_kernel(q_ref, k_ref, v_ref, g_ref, o_ref, m_sc, l_sc, acc_sc, *s_sc, tq, tk):
    i = pl.program_id(1)
    _attn_init(m_sc, l_sc, acc_sc)
    span = 2 * tk
    lower = slice(span, tq)

    def issue_scores(bufs, p, rows):
        for half, s_ref in enumerate(bufs):
            keys = k_ref[pl.ds(pl.multiple_of(p * span + half * tk, tk), tk), :]
            s_ref[rows, :] = _scores(q_ref[rows, :], keys)

    def softmax(bufs, p, rows, own_pair=None):
        allowed = None
        if own_pair is not None:
            qchunk = (rows.start + lax.broadcasted_iota(jnp.int32, (rows.stop - rows.start, 1), 0)) // CHUNK
            kchunk = (own_pair * span + lax.broadcasted_iota(jnp.int32, (1, span), 1)) // CHUNK
            allowed = kchunk <= qchunk
        s = jnp.concatenate([s_ref[rows, :] for s_ref in bufs], axis=-1)
        _softmax_pv(s, v_ref[pl.ds(pl.multiple_of(p * span, span), span), :], m_sc, l_sc, acc_sc, allowed,
                    rows.start)

    sa, sb = s_sc[:2], s_sc[2:]
    every = slice(0, tq)
    issue_scores(sa, 0, every)

    def earlier_tile(t, carry):
        issue_scores(sb, 2 * t + 1, every)
        softmax(sa, 2 * t, every)
        issue_scores(sa, 2 * t + 2, every)
        softmax(sb, 2 * t + 1, every)
        return carry

    lax.fori_loop(0, i, earlier_tile, 0)
    issue_scores(sb, 2 * i + 1, lower)
    softmax(sa, 2 * i, every, own_pair=0)
    softmax(sb, 2 * i + 1, lower, own_pair=1)
    inv = 1.0 / jnp.sum(l_sc[...], axis=-1, keepdims=True)
    o_ref[...] = (acc_sc[...] * inv * g_ref[...]).astype(o_ref.dtype)


def _attn_prompt_call(qh, kh, vh, g, tq, tk):
    heads, rows, _ = qh.shape
    assert tk % CHUNK == 0 and rows % tq == 0 and tq == 4 * tk
    per_head = lambda n, m, bufs: pl.BlockSpec((None, n, m), lambda hd, i: (hd, 0, 0),
                                               pipeline_mode=pl.Buffered(bufs))
    return pl.pallas_call(
        functools.partial(_attn_prompt_kernel, tq=tq, tk=tk),
        grid=(heads, rows // tq),
        in_specs=[pl.BlockSpec((None, tq, QK_HEAD), lambda hd, i: (hd, i, 0)),
                  per_head(rows, QK_HEAD, 1), per_head(rows, V_DIM, 2),
                  pl.BlockSpec((tq, V_DIM), lambda hd, i: (i, hd))],
        out_specs=pl.BlockSpec((tq, V_DIM), lambda hd, i: (i, hd)),
        out_shape=jax.ShapeDtypeStruct((rows, heads * V_DIM), BF16),
        scratch_shapes=_attn_scratch(tq, V_DIM) + [pltpu.VMEM((tq, tk), F32)] * 4,
        compiler_params=pltpu.CompilerParams(dimension_semantics=("arbitrary", "arbitrary"),
                                             vmem_limit_bytes=VMEM_LIMIT),
        name="attn_prompt",
    )(qh, kh, vh, g)


def _attn_sample_kernel(q_ref, pckv_ref, pkpet_ref, knew_ref, g_ref, wuv_ref, o_ref, m_sc, l_sc, acc_sc, *, tq, tk):
    q = q_ref[...].reshape(MLA_HEADS * tq, QCAT)
    _attn_init(m_sc, l_sc, acc_sc)
    past = pckv_ref.shape[0]
    q_lat = q[:, :KV_LORA]
    pair = q[:, KV_LORA:]
    q_pe = pair[:, :ROPE_DIM] + pair[:, ROPE_DIM:]

    def body(j, carry):
        r0 = pl.multiple_of(j * tk, tk)
        ckv = pckv_ref[pl.ds(r0, tk), :].astype(BF16)
        s = _scores(q_lat, ckv) + _dot(q_pe, pkpet_ref[:, pl.ds(r0, tk)].astype(BF16))
        _softmax_pv(s, ckv, m_sc, l_sc, acc_sc, None)
        return carry

    lax.fori_loop(0, past // tk, body, 0)
    knew = knew_ref[...]
    _softmax_pv(_scores(q, knew), knew[:, :KV_LORA], m_sc, l_sc, acc_sc, None)
    _attn_finish(l_sc, acc_sc, g_ref, wuv_ref, o_ref, tq)


def _attn_sample_call(qcat, past_ckv, past_kpe_t, layer, kcat, g, wuv_t, batch, length, tk):
    past = past_ckv.shape[2]
    assert past % CHUNK == 0 and length <= CHUNK and past % tk == 0
    rows = batch * length
    r8 = MLA_HEADS * length
    return pl.pallas_call(
        functools.partial(_attn_sample_kernel, tq=length, tk=tk),
        grid=(batch,),
        in_specs=[pl.BlockSpec((MLA_HEADS, length, QCAT), lambda b: (0, b, 0)),
                  pl.BlockSpec((None, None, past, KV_LORA), lambda b: (layer, b, 0, 0)),
                  pl.BlockSpec((None, None, ROPE_DIM, past), lambda b: (layer, b, 0, 0)),
                  pl.BlockSpec((length, QCAT), lambda b: (b, 0)),
                  pl.BlockSpec((length, MLA_WIDTH), lambda b: (b, 0)),
                  _const_spec(wuv_t.shape)],
        out_specs=pl.BlockSpec((length, MLA_WIDTH), lambda b: (b, 0)),
        out_shape=jax.ShapeDtypeStruct((rows, MLA_WIDTH), BF16),
        scratch_shapes=_attn_scratch(r8, KV_LORA),
        compiler_params=pltpu.CompilerParams(dimension_semantics=("arbitrary",), vmem_limit_bytes=VMEM_LIMIT),
        name="attn_sample",
    )(qcat, past_ckv, past_kpe_t, kcat, g, wuv_t)


def _ssd_constants(cl):
    hb = LANES // cl
    ltri = np.tril(np.ones((cl, cl), np.float32))
    exp_s = np.zeros((LANES, SSD_HEADS * cl), np.float32)
    exp_p = np.zeros((LANES, SSD_WIDTH), np.float32)
    for hd in range(SSD_HEADS):
        exp_s[hd, hd * cl:(hd + 1) * cl] = 1.0
        exp_p[hd, hd * SSD_HEAD_DIM:(hd + 1) * SSD_HEAD_DIM] = 1.0
    t = np.arange(cl)[:, None]
    s = np.tile(np.arange(cl), SSD_HEADS)[None, :]
    eye = (t == s).astype(np.float32)
    tril = (t >= s).astype(np.float32)
    hrow = np.repeat(np.arange(hb), cl)[:, None]
    hcol = np.repeat(np.arange(hb), SSD_HEAD_DIM)[None, :]
    bdiag = (hrow == hcol).astype(np.float32)
    bf = lambda a: jnp.asarray(a, BF16)
    return (bf(np.tile(ltri, (1, CUM_PIECES))), bf(np.tile(exp_s, (CUM_PIECES, 1))),
            bf(np.tile(exp_p, (WIDE_PIECES, 1))), jnp.asarray(eye), jnp.asarray(tril), bf(bdiag))


def _split(x, n):
    pieces = []
    for _ in range(n):
        p = x.astype(BF16)
        pieces.append(p)
        x = x - p.astype(F32)
    return pieces


def _expand(x, sel, n):
    return _dot(jnp.concatenate(_split(x, n), axis=-1), sel)


def _ssd_kernel(act_ref, dt_ref, z_ref, h0_ref, alog_ref, dsk_ref, nw_ref,
                ltri_ref, exps_ref, expp_ref, eye_ref, tril_ref, bdiag_ref,
                y_ref, hout_ref, h_sc, wide_sc, e_sc, *, cl, cps):
    step = pl.program_id(1)
    last = pl.num_programs(1) - 1
    rows = cl * cps

    @pl.when(step == 0)
    def _():
        h_sc[...] = h0_ref[...]

    dt = dt_ref[...]
    a_pieces = _split(dt * -jnp.exp(alog_ref[...]), CUM_PIECES)
    cums, totals = [], []
    for c in range(cps):
        stacked = jnp.concatenate([p[c * cl:(c + 1) * cl, :] for p in a_pieces], axis=0)
        cum_c = _dot(ltri_ref[...], stacked)
        cums.append(cum_c)
        totals.append(jnp.broadcast_to(cum_c[cl - 1:cl, :], cum_c.shape))
    cum = jnp.concatenate(cums, axis=0)
    total = jnp.concatenate(totals, axis=0)
    wide_sc[...] = _expand(jnp.concatenate([dt, jnp.exp(total - cum), jnp.exp(cum)], axis=0), expp_ref[...],
                           WIDE_PIECES)
    e_sc[...] = _expand(cum, exps_ref[...], CUM_PIECES)

    hb = LANES // cl
    gs = GROUP_WIDTH
    gm = HEADS_PER_GROUP * cl

    def chunk(c, carry):
        r0 = pl.multiple_of(c * cl, cl)
        xs = act_ref[pl.ds(r0, cl), 0:SSD_WIDTH]
        bm = act_ref[pl.ds(r0, cl), SSD_WIDTH:SSD_WIDTH + SSD_GROUPS * SSD_STATE]
        cm = act_ref[pl.ds(r0, cl), SSD_WIDTH + SSD_GROUPS * SSD_STATE:CONV_DIM]
        dt_w = wide_sc[pl.ds(r0, cl), :]
        dstate_w = wide_sc[pl.ds(pl.multiple_of(rows + r0, cl), cl), :]
        ecum_w = wide_sc[pl.ds(pl.multiple_of(2 * rows + r0, cl), cl), :]
        e = e_sc[pl.ds(r0, cl), :]
        f_row = jnp.sum(jnp.where(eye_ref[...] > 0.0, e, 0.0), axis=0, keepdims=True)
        decay = jnp.where(tril_ref[...] > 0.0, jnp.exp(e - f_row), 0.0)
        xd = xs * dt_w
        w_state = (xd * dstate_w).astype(BF16)
        xd16 = xd.astype(BF16)
        h_prev = h_sc[...]
        h16 = h_prev.astype(BF16)
        bdiag = bdiag_ref[...]
        y_parts, st_parts = [], []
        for g in range(SSD_GROUPS):
            bg = bm[:, g * SSD_STATE:(g + 1) * SSD_STATE].astype(BF16)
            cg = cm[:, g * SSD_STATE:(g + 1) * SSD_STATE].astype(BF16)
            cb = lax.dot_general(cg, jnp.concatenate([bg] * hb, axis=0), (((1,), (1,)), ((), ())),
                                 preferred_element_type=F32)
            y_off = _dot(cg, h16[:, g * gs:(g + 1) * gs]) * ecum_w[:, g * gs:(g + 1) * gs]
            diag = []
            for j in range(gm // LANES):
                m = (cb * decay[:, g * gm + j * LANES:g * gm + (j + 1) * LANES]).astype(BF16)
                c0 = g * gs + j * hb * SSD_HEAD_DIM
                xj = xd16[:, c0:c0 + hb * SSD_HEAD_DIM]
                diag.append(_dot(m, jnp.concatenate([xj] * hb, axis=0) * bdiag))
            y_parts.append(y_off + jnp.concatenate(diag, axis=-1))
            st_parts.append(lax.dot_general(bg, w_state[:, g * gs:(g + 1) * gs], (((0,), (0,)), ((), ())),
                                            preferred_element_type=F32))
        h_sc[...] = h_prev * ecum_w[cl - 1:cl, :] + jnp.concatenate(st_parts, axis=-1)
        y = jnp.concatenate(y_parts, axis=-1) + xs * dsk_ref[...]
        y = y * z_ref[pl.ds(r0, cl), :]
        normed = [_rms(y[:, g * gs:(g + 1) * gs], nw_ref[:, g * gs:(g + 1) * gs]) for g in range(SSD_GROUPS)]
        y_ref[pl.ds(r0, cl), :] = jnp.concatenate(normed, axis=-1).astype(y_ref.dtype)
        return carry

    lax.fori_loop(0, cps, chunk, 0, unroll=True)

    @pl.when(step == last)
    def _():
        hout_ref[...] = h_sc[...]


def _ssd_call(act, dt, z, h0_t, state_layer, w, batch, length, cps):
    cl = min(CHUNK, length)
    rows = cl * cps
    assert length % rows == 0 and LANES % cl == 0
    consts = _ssd_constants(cl)
    seq = lambda n: pl.BlockSpec((None, rows, n), lambda b, s: (b, s, 0))
    params = (w["a_log"], w["d_skip"], w["ssd_norm_w"])
    in_specs = ([seq(CONV_DIM), seq(LANES), seq(SSD_WIDTH),
                 pl.BlockSpec((None, None, SSD_STATE, SSD_WIDTH), lambda b, s: (state_layer, b, 0, 0))]
                + [pl.BlockSpec(a.shape, lambda b, s: (0, 0)) for a in params + consts])
    out_shape = (jax.ShapeDtypeStruct((batch, length, SSD_WIDTH), BF16),
                 jax.ShapeDtypeStruct((batch, SSD_STATE, SSD_WIDTH), F32))
    out_specs = (seq(SSD_WIDTH), pl.BlockSpec((None, SSD_STATE, SSD_WIDTH), lambda b, s: (b, 0, 0)))
    return pl.pallas_call(
        functools.partial(_ssd_kernel, cl=cl, cps=cps),
        grid=(batch, length // rows), in_specs=in_specs, out_specs=out_specs, out_shape=out_shape,
        scratch_shapes=[pltpu.VMEM((SSD_STATE, SSD_WIDTH), F32), pltpu.VMEM((3 * rows, SSD_WIDTH), F32),
                        pltpu.VMEM((rows, SSD_HEADS * cl), F32)],
        compiler_params=pltpu.CompilerParams(dimension_semantics=("arbitrary", "arbitrary"),
                                             vmem_limit_bytes=VMEM_LIMIT),
        name="ssd",
    )(act.reshape(batch, length, CONV_DIM), dt.reshape(batch, length, LANES),
      z.reshape(batch, length, SSD_WIDTH), h0_t, *params, *consts)


def _post_kernel(mla_ref, ssd_ref, x_ref, wa_ref, wb_ref, fw_ref, o_ref, *, final):
    y = x_ref[...] + _dot(mla_ref[...], wa_ref[...]) + _dot(ssd_ref[...], wb_ref[...])
    o_ref[...] = _rms(y, fw_ref[...]) if final else y


def _post_call(mla, ssd, x, w, final_norm_w, final, tr):
    rows, d = x.shape
    row = lambda n: pl.BlockSpec((tr, n), lambda i: (i, 0))
    return pl.pallas_call(
        functools.partial(_post_kernel, final=final),
        grid=(rows // tr,),
        in_specs=[row(MLA_WIDTH), row(SSD_WIDTH), row(d), _const_spec(w["w_out_mla"].shape),
                  _const_spec(w["w_out_ssd"].shape), _const_spec(final_norm_w.shape)],
        out_specs=row(d), out_shape=jax.ShapeDtypeStruct((rows, d), F32),
        compiler_params=pltpu.CompilerParams(dimension_semantics=("arbitrary",), vmem_limit_bytes=VMEM_LIMIT),
        name="post",
    )(mla, ssd, x, w["w_out_mla"], w["w_out_ssd"], final_norm_w)


def _swap_halves(wcols):
    k, n = wcols.shape
    blocks = wcols.reshape(k, n // ROPE_DIM, 2, ROPE_DIM // 2)
    return blocks[:, :, ::-1, :].reshape(k, n)


def _layer_weights(i, norm_w, w_in, q_norm_w, w_uq, kv_norm_w, w_uk, w_uv, conv_w, conv_b, dt_bias, a_log, d_skip,
                   ssd_norm_w, w_out):
    d = w_in.shape[1]
    offs = np.cumsum((0, Q_LORA, KV_LORA, ROPE_DIM, MLA_WIDTH, SSD_WIDTH, CONV_DIM, SSD_HEADS))
    col = lambda j: w_in[i][:, offs[j]:offs[j + 1]]
    w_kpe = col(2)
    w_kpe_sw = _swap_halves(w_kpe)
    w_sm = jnp.concatenate([w_kpe, w_kpe, w_kpe_sw, w_kpe_sw], axis=1)
    w_qdt = jnp.concatenate([col(0), col(6), jnp.zeros((d, LANES - SSD_HEADS), F32)], axis=1)
    uq = w_uq[i].reshape(Q_LORA, MLA_HEADS, NOPE_DIM + ROPE_DIM)
    uq_pe = uq[:, :, NOPE_DIM:].reshape(Q_LORA, MLA_HEADS * ROPE_DIM)
    pad_heads = lambda v: jnp.concatenate([v, jnp.zeros((LANES - SSD_HEADS,), F32)])[None, :]
    return {
        "depth": w_in.shape[0],
        "norm_w": norm_w[i][None, :],
        "w_q": w_qdt.astype(BF16), "w_kv": col(1).astype(BF16), "w_g": col(3).astype(BF16),
        "w_z": col(4).astype(BF16), "w_xbc": col(5).astype(BF16), "w_sm": w_sm.astype(BF16),
        "q_norm_w": q_norm_w[i][None, :],
        "w_uq_nope": uq[:, :, :NOPE_DIM].reshape(Q_LORA, MLA_HEADS * NOPE_DIM).astype(BF16),
        "w_uq_pe": uq_pe.astype(BF16), "w_uq_pe_sw": _swap_halves(uq_pe).astype(BF16),
        "kv_norm_w": kv_norm_w[i][None, :],
        "w_uk_t": jnp.transpose(w_uk[i], (1, 2, 0)).astype(BF16),
        "w_uv_t": jnp.transpose(w_uv[i], (1, 0, 2)).astype(BF16),
        "w_uk_flat": w_uk[i].reshape(KV_LORA, MLA_HEADS * NOPE_DIM).astype(BF16),
        "w_uv_flat": w_uv[i].reshape(KV_LORA, MLA_HEADS * V_DIM).astype(BF16),
        "conv_w": conv_w[i], "conv_b": conv_b[i][None, :],
        "dt_bias": pad_heads(dt_bias[i]), "a_log": pad_heads(a_log[i]),
        "d_skip": jnp.repeat(d_skip[i], SSD_HEAD_DIM)[None, :],
        "ssd_norm_w": ssd_norm_w[i][None, :],
        "w_out_mla": w_out[i][:MLA_WIDTH].astype(BF16), "w_out_ssd": w_out[i][MLA_WIDTH:].astype(BF16),
    }


def _rope_tables(past, length, batch):
    half = ROPE_DIM // 2
    inv = 1.0 / (ROPE_THETA ** (np.arange(half, dtype=np.float64) / half))
    ang = (past + np.arange(length, dtype=np.float64))[:, None] * inv[None, :]
    cos, sin = np.cos(ang), np.sin(ang)
    reps = LANES // ROPE_DIM
    cos2 = np.tile(np.concatenate([cos, cos], axis=-1), (batch, reps))
    sin2 = np.tile(np.concatenate([-sin, sin], axis=-1), (batch, reps))
    return jnp.asarray(cos2, F32), jnp.asarray(sin2, F32)


def _state_to_lanes(h):
    b = h.shape[0]
    return jnp.transpose(h, (0, 3, 1, 2)).reshape(b, SSD_STATE, SSD_WIDTH)


def _state_from_lanes(ht):
    b = ht.shape[0]
    return jnp.transpose(ht.reshape(b, SSD_STATE, SSD_HEADS, SSD_HEAD_DIM), (0, 2, 3, 1))


def _pick(n, prefs):
    for p in prefs:
        if n % p == 0:
            return p
    return n


def _layer(x, batch, length, layer, state_layer, caches, conv_state, ssm_state_t, stacked, w, tables, final_norm_w):
    rows = batch * length
    tr = _pick(rows, (512, 256, 128))
    absorbed = caches is not None
    *attn_in, ckv, kpe, g, z, act, dt, conv_new = _pre_call(x, conv_state, layer, state_layer, stacked, w, tables[0],
                                                            tables[1], batch, length, tr, absorbed)
    if absorbed:
        qcat, kcat = attn_in
        mla = _attn_sample_call(qcat, caches[0], caches[1], layer, kcat, g, w["w_uv_t"], batch, length,
                                _pick(caches[0].shape[2], (2048, 1024, 512, 256, 128, 64)))
    else:
        assert batch == 1
        tq = _pick(length, (2048, 1024, 512))
        mla = _attn_prompt_call(*attn_in, g, tq, min(tq // 4, 512))
    cl = min(CHUNK, length)
    ssd, h_t = _ssd_call(act, dt, z, ssm_state_t, state_layer, w, batch, length, _pick(length // cl, (16, 8, 4, 2, 1)))
    y = _post_call(mla, ssd.reshape(rows, SSD_WIDTH), x, w, final_norm_w, layer == w["depth"] - 1,
                   _pick(rows, (1024, 512, 256, 128)))
    return y, (ckv, kpe), conv_new, h_t


def kernel(x_prompt, x_sample, cache_ckv, cache_kpe, state_conv, state_ssm, norm_w, w_in, q_norm_w, w_uq,
           kv_norm_w, w_uk, w_uv, conv_w, conv_b, dt_bias, a_log, d_skip, ssd_norm_w, w_out, final_norm_w):
    depth = w_in.shape[0]
    bp, lp, d = x_prompt.shape
    bs, ls, _ = x_sample.shape
    past = cache_ckv.shape[2]
    tab_p = _rope_tables(0, lp, bp)
    tab_s = _rope_tables(past, ls, bs)
    fw = final_norm_w[None, :]
    yp, ys = x_prompt.reshape(bp * lp, d), x_sample.reshape(bs * ls, d)
    zero_conv = jnp.zeros((1, bp, CONV_W - 1, CONV_DIM), F32)
    zero_ssm = jnp.zeros((1, bp, SSD_STATE, SSD_WIDTH), F32)
    ssm_t = _state_to_lanes(state_ssm.reshape((depth * bs,) + state_ssm.shape[2:]))
    ssm_t = ssm_t.reshape(depth, bs, SSD_STATE, SSD_WIDTH)
    kpe_t = jnp.swapaxes(cache_kpe, 2, 3)
    kv_p = (jnp.zeros((depth, bp * lp, KV_LORA), F32), jnp.zeros((depth, bp * lp, ROPE_DIM), F32))
    kv_s = (jnp.zeros((depth, bs * ls, KV_LORA), F32), jnp.zeros((depth, bs * ls, ROPE_DIM), F32))
    conv_p, conv_s, ssm_p, ssm_s = [], [], [], []
    for i in range(depth):
        w = _layer_weights(i, norm_w, w_in, q_norm_w, w_uq, kv_norm_w, w_uk, w_uv, conv_w, conv_b, dt_bias, a_log,
                           d_skip, ssd_norm_w, w_out)
        yp, kv_p, c_new, h_new = _layer(yp, bp, lp, i, 0, None, zero_conv, zero_ssm, kv_p, w, tab_p, fw)
        conv_p.append(c_new)
        ssm_p.append(_state_from_lanes(h_new))
        ys, kv_s, c_new, h_new = _layer(ys, bs, ls, i, i, (cache_ckv, kpe_t), state_conv, ssm_t, kv_s, w,
                                        tab_s, fw)
        conv_s.append(c_new)
        ssm_s.append(_state_from_lanes(h_new))
    return (yp.reshape(bp, lp, d), ys.reshape(bs, ls, d),
            kv_p[0].reshape(depth, bp, lp, KV_LORA), kv_p[1].reshape(depth, bp, lp, ROPE_DIM),
            jnp.stack(conv_p), jnp.stack(ssm_p),
            kv_s[0].reshape(depth, bs, ls, KV_LORA), kv_s[1].reshape(depth, bs, ls, ROPE_DIM),
            jnp.stack(conv_s), jnp.stack(ssm_s))
```

```python
import functools

import numpy as np
import jax
import jax.numpy as jnp
from jax import lax
from jax.experimental import pallas as pl
from jax.experimental.pallas import tpu as pltpu

F32 = jnp.float32
BF16 = jnp.bfloat16

EPS = 1e-6
CHUNK = 64
MLA_HEADS = 8
Q_LORA = 384
KV_LORA = 256
NOPE_DIM = 128
ROPE_DIM = 64
V_DIM = 128
ROPE_THETA = 10000.0
ATTN_SCALE = (NOPE_DIM + ROPE_DIM) ** -0.5
Q_SCALE = ATTN_SCALE * float(np.log2(np.e))
SSD_HEADS = 16
SSD_HEAD_DIM = 64
SSD_GROUPS = 2
SSD_STATE = 128
CONV_W = 4
SSD_WIDTH = SSD_HEADS * SSD_HEAD_DIM
GROUP_WIDTH = SSD_WIDTH // SSD_GROUPS
HEADS_PER_GROUP = SSD_HEADS // SSD_GROUPS
CONV_DIM = SSD_WIDTH + 2 * SSD_GROUPS * SSD_STATE
MLA_WIDTH = MLA_HEADS * V_DIM
QCAT = KV_LORA + 2 * ROPE_DIM
QK_HEAD = NOPE_DIM + 2 * ROPE_DIM
V_SUM = V_DIM + 128

CUM_PIECES = 3
WIDE_PIECES = 2

LANES = 128
SUBLANES = 8
VMEM_LIMIT = 56 * 1024 * 1024
ATTN_VMEM_LIMIT = 61 * 1024 * 1024


def _dot(a, b):
    return jnp.dot(a, b, preferred_element_type=F32)


def _rms(x, w):
    return x * lax.rsqrt(jnp.mean(x * x, axis=-1, keepdims=True) + EPS) * w


def _silu(x):
    half = 0.5 * x
    return half + half * jnp.tanh(half)


def _const_spec(shape):
    nd = len(shape)
    return pl.BlockSpec(shape, lambda *_: (0,) * nd)


def _pre_kernel(*refs, seq_rows, tiles_per_seq, n_alias, absorbed):
    refs = refs[n_alias:]
    (x_ref, cst_ref, nw_ref, wq_ref, wkv_ref, wg_ref, wz_ref, wxbc_ref, wsm_ref, qnw_ref, wuqn_ref, wuqp_ref,
     wuqps_ref, kvnw_ref, cw_ref, cb_ref, dtb_ref) = refs[:17]
    n_w, n_out = (1, 2) if absorbed else (2, 3)
    attn_w = refs[17:17 + n_w]
    cos_ref, sin_ref = refs[17 + n_w:19 + n_w]
    attn_out = refs[19 + n_w:19 + n_w + n_out]
    ckv_ref, kpe_ref, g_ref, z_ref, xbc_ref, dt_ref, cnew_ref, buf_sc = refs[19 + n_w + n_out:]
    keep = CONV_W - 1
    base = SUBLANES - keep
    if tiles_per_seq > 1:
        @pl.when(pl.program_id(0) % tiles_per_seq == 0)
        def _():
            buf_sc[0:base, :] = jnp.zeros((base, buf_sc.shape[1]), F32)
            buf_sc[base:SUBLANES, :] = cst_ref[0]

    x = x_ref[...]
    h = _rms(x, nw_ref[...]).astype(BF16)
    xbc = _dot(h, wxbc_ref[...])
    g_ref[...] = _silu(_dot(h, wg_ref[...]))
    z_ref[...] = _silu(_dot(h, wz_ref[...]))

    cos2 = cos_ref[...]
    sin2 = sin_ref[...]
    sm = _dot(h, wsm_ref[...])
    kpe2 = sm[:, 0:LANES] * cos2 + sm[:, LANES:2 * LANES] * sin2
    kpe_ref[...] = kpe2[:, :ROPE_DIM]
    ckv = _rms(_dot(h, wkv_ref[...]), kvnw_ref[...])
    ckv_ref[...] = ckv

    qdt = _dot(h, wq_ref[...])
    dt_in = qdt[:, Q_LORA:Q_LORA + LANES] + dtb_ref[...]
    dt_ref[...] = jnp.maximum(dt_in, 0.0) + jnp.log1p(jnp.exp(-jnp.abs(dt_in)))
    qn = _rms(qdt[:, :Q_LORA], qnw_ref[...]).astype(BF16)
    qnope = _dot(qn, wuqn_ref[...])
    pairs = MLA_HEADS // 2
    cos_all = jnp.concatenate([cos2] * pairs, axis=-1)
    sin_all = jnp.concatenate([sin2] * pairs, axis=-1)
    qpe = _dot(qn, wuqp_ref[...]) * cos_all + _dot(qn, wuqps_ref[...]) * sin_all
    lane = lax.broadcasted_iota(jnp.int32, (x.shape[0], LANES), 1)
    ckv16 = ckv.astype(BF16)
    if absorbed:
        (wuk_ref,) = attn_w
        q_ref, kcat_ref = attn_out
        kcat_ref[...] = jnp.concatenate([ckv16, kpe2.astype(BF16)], axis=-1)
    else:
        wukf_ref, wuvf_ref = attn_w
        q_ref, kh_ref, vh_ref = attn_out
        k_nope = _dot(ckv16, wukf_ref[...])
        v_all = _dot(ckv16, wuvf_ref[...])
    for hd in range(MLA_HEADS):
        cols = slice(hd * NOPE_DIM, (hd + 1) * NOPE_DIM)
        pair = qpe[:, (hd // 2) * LANES:(hd // 2 + 1) * LANES] * Q_SCALE
        own_half = (lane < ROPE_DIM) if hd % 2 == 0 else (lane >= ROPE_DIM)
        q_pe = jnp.where(own_half, pair, 0.0)
        if absorbed:
            q_main = _dot(qnope[:, cols].astype(BF16), wuk_ref[hd]) * Q_SCALE
        else:
            q_main = qnope[:, cols] * Q_SCALE
            kh_ref[hd] = jnp.concatenate([k_nope[:, cols], kpe2], axis=-1).astype(BF16)
            vh_ref[hd] = jnp.concatenate([v_all[:, hd * V_DIM:(hd + 1) * V_DIM], jnp.ones_like(kpe2)],
                                         axis=-1).astype(BF16)
        q_ref[hd] = jnp.concatenate([q_main, q_pe], axis=-1).astype(BF16)

    for s in range(x.shape[0] // seq_rows):
        if tiles_per_seq == 1:
            buf_sc[0:base, :] = jnp.zeros((base, buf_sc.shape[1]), F32)
            buf_sc[base:SUBLANES, :] = cst_ref[s]
        buf_sc[SUBLANES:SUBLANES + seq_rows, :] = xbc[s * seq_rows:(s + 1) * seq_rows, :]
        full = buf_sc[...]
        conv = cb_ref[...] + cw_ref[keep:CONV_W, :] * full[SUBLANES:, :]
        for k in range(1, CONV_W):
            shifted = pltpu.roll(full, k, axis=0)[SUBLANES:, :]
            conv = conv + cw_ref[keep - k:CONV_W - k, :] * shifted
        xbc_ref[s * seq_rows:(s + 1) * seq_rows, :] = _silu(conv)
        tail = buf_sc[base + seq_rows:SUBLANES + seq_rows, :]
        buf_sc[base:SUBLANES, :] = tail
        cnew_ref[s] = tail


def _pre_call(x, conv_state, layer, state_layer, stacked, w, cos2, sin2, batch, length, tr, absorbed):
    rows, d = x.shape
    depth = w["depth"]
    seq_rows = min(length, tr)
    tiles_per_seq = length // seq_rows
    nseq = tr // seq_rows
    assert rows == batch * length and length % seq_rows == 0 and tr % seq_rows == 0 and seq_rows >= CONV_W - 1
    row = lambda n: pl.BlockSpec((tr, n), lambda i: (i, 0))
    layer_row = lambda n: pl.BlockSpec((None, tr, n), lambda i: (layer, i, 0))
    heads = lambda n: pl.BlockSpec((MLA_HEADS, tr, n), lambda i: (0, i, 0))
    head_shape = lambda n: jax.ShapeDtypeStruct((MLA_HEADS, rows, n), BF16)
    if absorbed:
        attn_w = (w["w_uk_t"],)
        attn_shape = (head_shape(QCAT), jax.ShapeDtypeStruct((rows, QCAT), BF16))
        attn_specs = (heads(QCAT), row(QCAT))
    else:
        attn_w = (w["w_uk_flat"], w["w_uv_flat"])
        attn_shape = (head_shape(QK_HEAD), head_shape(QK_HEAD), head_shape(V_SUM))
        attn_specs = (heads(QK_HEAD), heads(QK_HEAD), heads(V_SUM))
    weights = (w["norm_w"], w["w_q"], w["w_kv"], w["w_g"], w["w_z"], w["w_xbc"], w["w_sm"],
               w["q_norm_w"], w["w_uq_nope"], w["w_uq_pe"], w["w_uq_pe_sw"], w["kv_norm_w"],
               w["conv_w"], w["conv_b"], w["dt_bias"]) + attn_w
    aliased = tuple(stacked)
    in_specs = ([pl.BlockSpec(memory_space=pl.ANY)] * len(aliased)
                + [row(d), pl.BlockSpec((None, nseq, CONV_W - 1, CONV_DIM),
                                        lambda i: (state_layer, i // tiles_per_seq, 0, 0))]
                + [_const_spec(a.shape) for a in weights] + [row(LANES), row(LANES)])
    out_shape = attn_shape + (
        jax.ShapeDtypeStruct((depth, rows, KV_LORA), F32),
        jax.ShapeDtypeStruct((depth, rows, ROPE_DIM), F32),
        jax.ShapeDtypeStruct((rows, MLA_WIDTH), F32),
        jax.ShapeDtypeStruct((rows, SSD_WIDTH), F32),
        jax.ShapeDtypeStruct((rows, CONV_DIM), F32),
        jax.ShapeDtypeStruct((rows, LANES), F32),
        jax.ShapeDtypeStruct((batch, CONV_W - 1, CONV_DIM), F32),
    )
    out_specs = attn_specs + (
        layer_row(KV_LORA), layer_row(ROPE_DIM), row(MLA_WIDTH), row(SSD_WIDTH), row(CONV_DIM),
        row(LANES), pl.BlockSpec((nseq, CONV_W - 1, CONV_DIM), lambda i: (i // tiles_per_seq, 0, 0)),
    )
    n_attn = len(attn_shape)
    return pl.pallas_call(
        functools.partial(_pre_kernel, seq_rows=seq_rows, tiles_per_seq=tiles_per_seq, n_alias=len(aliased),
                          absorbed=absorbed),
        grid=(rows // tr,), in_specs=in_specs, out_specs=out_specs, out_shape=out_shape,
        scratch_shapes=[pltpu.VMEM((seq_rows + SUBLANES, CONV_DIM), F32)],
        input_output_aliases={k: n_attn + k for k in range(len(aliased))},
        compiler_params=pltpu.CompilerParams(dimension_semantics=("arbitrary",), vmem_limit_bytes=VMEM_LIMIT),
        name="pre",
    )(*aliased, x, conv_state, *weights, cos2, sin2)


def _scores(q, kblk):
    return lax.dot_general(q, kblk, (((1,), (1,)), ((), ())), preferred_element_type=F32)


def _softmax_pv(s, v, m_sc, l_sc, acc_sc, allowed, row0=0):
    rows = slice(row0, row0 + s.shape[0])
    if allowed is not None:
        s = jnp.where(allowed, s, -jnp.inf)
    tk = s.shape[1]
    m_prev = m_sc[rows, :]
    if tk % LANES == 0:
        chunks = [s[:, c * LANES:(c + 1) * LANES] for c in range(tk // LANES)]
        m_new = jnp.maximum(m_prev, jnp.max(functools.reduce(jnp.maximum, chunks), axis=-1, keepdims=True))
        ps = [jnp.exp2(c - m_new) for c in chunks]
        psum = functools.reduce(jnp.add, ps)
        p = jnp.concatenate(ps, axis=-1)
    else:
        m_new = jnp.maximum(m_prev, jnp.max(s, axis=-1, keepdims=True))
        p = jnp.exp2(s - m_new[:, :1])
        psum = jnp.sum(p, axis=-1, keepdims=True) * (1.0 / LANES)
    alpha = jnp.exp2(m_prev - m_new)
    l_sc[rows, :] = alpha * l_sc[rows, :] + psum
    pv = _dot(p.astype(BF16), v)
    acc_sc[rows, :] = jnp.concatenate([alpha] * (acc_sc.shape[1] // LANES), axis=-1) * acc_sc[rows, :] + pv
    m_sc[rows, :] = m_new


def _attn_finish(l_sc, acc_sc, g_ref, wuv_ref, o_ref, tq):
    for hd in range(MLA_HEADS):
        rows = slice(hd * tq, (hd + 1) * tq)
        cols = slice(hd * V_DIM, (hd + 1) * V_DIM)
        inv = 1.0 / jnp.sum(l_sc[rows, :], axis=-1, keepdims=True)
        o = _dot((acc_sc[rows, :] * inv).astype(BF16), wuv_ref[hd])
        o_ref[:, cols] = (o * g_ref[:, cols]).astype(o_ref.dtype)


def _attn_init(m_sc, l_sc, acc_sc):
    m_sc[...] = jnp.full(m_sc.shape, -jnp.inf, F32)
    l_sc[...] = jnp.zeros(l_sc.shape, F32)
    acc_sc[...] = jnp.zeros(acc_sc.shape, F32)


def _attn_scratch(rows, width):
    return [pltpu.VMEM((rows, LANES), F32), pltpu.VMEM((rows, LANES), F32), pltpu.VMEM((rows, width), F32)]


def _softmax_pv_sums(s, v1, m_sc, acc_sc, allowed, row0):
    rows = slice(row0, row0 + s.shape[0])
    if allowed is not None:
        s = jnp.where(allowed, s, -jnp.inf)
    chunks = [s[:, c * LANES:(c + 1) * LANES] for c in range(s.shape[1] // LANES)]
    m_prev = m_sc[rows, :]
    m_new = jnp.maximum(m_prev, jnp.max(functools.reduce(jnp.maximum, chunks), axis=-1, keepdims=True))
    p = jnp.concatenate([jnp.exp2((c - m_new).astype(BF16)) for c in chunks], axis=-1)
    alpha = jnp.exp2(m_prev - m_new)
    acc_sc[rows, :] = jnp.concatenate([alpha] * (acc_sc.shape[1] // LANES), axis=-1) * acc_sc[rows, :] + _dot(p, v1)
    m_sc[rows, :] = m_new


def _attn_prompt_kernel(q_ref, k_ref, v_ref, g_ref, o_ref, m_sc, acc_sc, *s_sc, tq, tk):
    i = pl.program_id(1)
    m_sc[...] = jnp.full(m_sc.shape, -jnp.inf, F32)
    acc_sc[...] = jnp.zeros(acc_sc.shape, F32)
    span = 2 * tk
    lower = slice(span, tq)

    def issue_scores(bufs, p, rows):
        for half, s_ref in enumerate(bufs):
            keys = k_ref[pl.ds(pl.multiple_of(p * span + half * tk, tk), tk), :]
            s_ref[rows, :] = _scores(q_ref[rows, :], keys)

    def softmax(bufs, p, rows, own_pair=None):
        allowed = None
        if own_pair is not None:
            qchunk = (rows.start + lax.broadcasted_iota(jnp.int32, (rows.stop - rows.start, 1), 0)) // CHUNK
            kchunk = (own_pair * span + lax.broadcasted_iota(jnp.int32, (1, span), 1)) // CHUNK
            allowed = kchunk <= qchunk
        s = jnp.concatenate([s_ref[rows, :] for s_ref in bufs], axis=-1)
        _softmax_pv_sums(s, v_ref[pl.ds(pl.multiple_of(p * span, span), span), :], m_sc, acc_sc, allowed,
                         rows.start)

    sa, sb = s_sc[:2], s_sc[2:]
    every = slice(0, tq)
    issue_scores(sa, 0, every)

    def earlier_tile(t, carry):
        issue_scores(sb, 2 * t + 1, every)
        softmax(sa, 2 * t, every)
        issue_scores(sa, 2 * t + 2, every)
        softmax(sb, 2 * t + 1, every)
        return carry

    lax.fori_loop(0, i, earlier_tile, 0)
    issue_scores(sb, 2 * i + 1, lower)
    softmax(sa, 2 * i, every, own_pair=0)
    softmax(sb, 2 * i + 1, lower, own_pair=1)
    o_ref[...] = (acc_sc[:, :V_DIM] / acc_sc[:, V_DIM:] * g_ref[...]).astype(o_ref.dtype)


def _attn_prompt_call(qh, kh, vh, g, tq, tk):
    heads, rows, _ = qh.shape
    assert tk % CHUNK == 0 and rows % tq == 0 and tq == 4 * tk
    per_head = lambda n, m: pl.BlockSpec((None, n, m), lambda hd, i: (hd, 0, 0), pipeline_mode=pl.Buffered(1))
    return pl.pallas_call(
        functools.partial(_attn_prompt_kernel, tq=tq, tk=tk),
        grid=(heads, rows // tq),
        in_specs=[pl.BlockSpec((None, tq, QK_HEAD), lambda hd, i: (hd, i, 0)),
                  per_head(rows, QK_HEAD), per_head(rows, V_SUM),
                  pl.BlockSpec((tq, V_DIM), lambda hd, i: (i, hd))],
        out_specs=pl.BlockSpec((tq, V_DIM), lambda hd, i: (i, hd)),
        out_shape=jax.ShapeDtypeStruct((rows, heads * V_DIM), BF16),
        scratch_shapes=[pltpu.VMEM((tq, LANES), F32), pltpu.VMEM((tq, V_SUM), F32)] + [pltpu.VMEM((tq, tk), F32)] * 4,
        compiler_params=pltpu.CompilerParams(dimension_semantics=("arbitrary", "arbitrary"),
                                             vmem_limit_bytes=ATTN_VMEM_LIMIT),
        name="attn_prompt",
    )(qh, kh, vh, g)


def _attn_sample_kernel(q_ref, pckv_ref, pkpet_ref, knew_ref, g_ref, wuv_ref, o_ref, m_sc, l_sc, acc_sc, *, tq, tk):
    q = q_ref[...].reshape(MLA_HEADS * tq, QCAT)
    _attn_init(m_sc, l_sc, acc_sc)
    past = pckv_ref.shape[0]
    q_lat = q[:, :KV_LORA]
    pair = q[:, KV_LORA:]
    q_pe = pair[:, :ROPE_DIM] + pair[:, ROPE_DIM:]

    def body(j, carry):
        r0 = pl.multiple_of(j * tk, tk)
        ckv = pckv_ref[pl.ds(r0, tk), :].astype(BF16)
        s = _scores(q_lat, ckv) + _dot(q_pe, pkpet_ref[:, pl.ds(r0, tk)].astype(BF16))
        _softmax_pv(s, ckv, m_sc, l_sc, acc_sc, None)
        return carry

    lax.fori_loop(0, past // tk, body, 0)
    knew = knew_ref[...]
    _softmax_pv(_scores(q, knew), knew[:, :KV_LORA], m_sc, l_sc, acc_sc, None)
    _attn_finish(l_sc, acc_sc, g_ref, wuv_ref, o_ref, tq)


def _attn_sample_call(qcat, past_ckv, past_kpe_t, layer, kcat, g, wuv_t, batch, length, tk):
    past = past_ckv.shape[2]
    assert past % CHUNK == 0 and length <= CHUNK and past % tk == 0
    rows = batch * length
    r8 = MLA_HEADS * length
    return pl.pallas_call(
        functools.partial(_attn_sample_kernel, tq=length, tk=tk),
        grid=(batch,),
        in_specs=[pl.BlockSpec((MLA_HEADS, length, QCAT), lambda b: (0, b, 0)),
                  pl.BlockSpec((None, None, past, KV_LORA), lambda b: (layer, b, 0, 0)),
                  pl.BlockSpec((None, None, ROPE_DIM, past), lambda b: (layer, b, 0, 0)),
                  pl.BlockSpec((length, QCAT), lambda b: (b, 0)),
                  pl.BlockSpec((length, MLA_WIDTH), lambda b: (b, 0)),
                  _const_spec(wuv_t.shape)],
        out_specs=pl.BlockSpec((length, MLA_WIDTH), lambda b: (b, 0)),
        out_shape=jax.ShapeDtypeStruct((rows, MLA_WIDTH), BF16),
        scratch_shapes=_attn_scratch(r8, KV_LORA),
        compiler_params=pltpu.CompilerParams(dimension_semantics=("arbitrary",), vmem_limit_bytes=VMEM_LIMIT),
        name="attn_sample",
    )(qcat, past_ckv, past_kpe_t, kcat, g, wuv_t)


def _ssd_constants(cl):
    hb = LANES // cl
    ltri = np.tril(np.ones((cl, cl), np.float32))
    exp_s = np.zeros((LANES, SSD_HEADS * cl), np.float32)
    exp_p = np.zeros((LANES, SSD_WIDTH), np.float32)
    for hd in range(SSD_HEADS):
        exp_s[hd, hd * cl:(hd + 1) * cl] = 1.0
        exp_p[hd, hd * SSD_HEAD_DIM:(hd + 1) * SSD_HEAD_DIM] = 1.0
    t = np.arange(cl)[:, None]
    s = np.tile(np.arange(cl), SSD_HEADS)[None, :]
    eye = (t == s).astype(np.float32)
    tril = (t >= s).astype(np.float32)
    hrow = np.repeat(np.arange(hb), cl)[:, None]
    hcol = np.repeat(np.arange(hb), SSD_HEAD_DIM)[None, :]
    bdiag = (hrow == hcol).astype(np.float32)
    bf = lambda a: jnp.asarray(a, BF16)
    return (bf(np.tile(ltri, (1, CUM_PIECES))), bf(np.tile(exp_s, (CUM_PIECES, 1))),
            bf(np.tile(exp_p, (WIDE_PIECES, 1))), jnp.asarray(eye), jnp.asarray(tril), bf(bdiag))


def _split(x, n):
    pieces = []
    for _ in range(n):
        p = x.astype(BF16)
        pieces.append(p)
        x = x - p.astype(F32)
    return pieces


def _expand(x, sel, n):
    return _dot(jnp.concatenate(_split(x, n), axis=-1), sel)


def _ssd_kernel(act_ref, dt_ref, z_ref, h0_ref, alog_ref, dsk_ref, nw_ref,
                ltri_ref, exps_ref, expp_ref, eye_ref, tril_ref, bdiag_ref,
                y_ref, hout_ref, h_sc, wide_sc, e_sc, *, cl, cps):
    step = pl.program_id(1)
    last = pl.num_programs(1) - 1
    rows = cl * cps

    @pl.when(step == 0)
    def _():
        h_sc[...] = h0_ref[...]

    dt = dt_ref[...]
    a_pieces = _split(dt * -jnp.exp(alog_ref[...]), CUM_PIECES)
    cums, totals = [], []
    for c in range(cps):
        stacked = jnp.concatenate([p[c * cl:(c + 1) * cl, :] for p in a_pieces], axis=0)
        cum_c = _dot(ltri_ref[...], stacked)
        cums.append(cum_c)
        totals.append(jnp.broadcast_to(cum_c[cl - 1:cl, :], cum_c.shape))
    cum = jnp.concatenate(cums, axis=0)
    total = jnp.concatenate(totals, axis=0)
    wide_sc[...] = _expand(jnp.concatenate([dt, jnp.exp(total - cum), jnp.exp(cum)], axis=0), expp_ref[...],
                           WIDE_PIECES)
    e_sc[...] = _expand(cum, exps_ref[...], CUM_PIECES)

    hb = LANES // cl
    gs = GROUP_WIDTH
    gm = HEADS_PER_GROUP * cl

    def chunk(c, carry):
        r0 = pl.multiple_of(c * cl, cl)
        xs = act_ref[pl.ds(r0, cl), 0:SSD_WIDTH]
        bm = act_ref[pl.ds(r0, cl), SSD_WIDTH:SSD_WIDTH + SSD_GROUPS * SSD_STATE]
        cm = act_ref[pl.ds(r0, cl), SSD_WIDTH + SSD_GROUPS * SSD_STATE:CONV_DIM]
        dt_w = wide_sc[pl.ds(r0, cl), :]
        dstate_w = wide_sc[pl.ds(pl.multiple_of(rows + r0, cl), cl), :]
        ecum_w = wide_sc[pl.ds(pl.multiple_of(2 * rows + r0, cl), cl), :]
        e = e_sc[pl.ds(r0, cl), :]
        f_row = jnp.sum(jnp.where(eye_ref[...] > 0.0, e, 0.0), axis=0, keepdims=True)
        decay = jnp.where(tril_ref[...] > 0.0, jnp.exp(e - f_row), 0.0)
        xd = xs * dt_w
        w_state = (xd * dstate_w).astype(BF16)
        xd16 = xd.astype(BF16)
        h_prev = h_sc[...]
        h16 = h_prev.astype(BF16)
        bdiag = bdiag_ref[...]
        y_parts, st_parts = [], []
        for g in range(SSD_GROUPS):
            bg = bm[:, g * SSD_STATE:(g + 1) * SSD_STATE].astype(BF16)
            cg = cm[:, g * SSD_STATE:(g + 1) * SSD_STATE].astype(BF16)
            cb = lax.dot_general(cg, jnp.concatenate([bg] * hb, axis=0), (((1,), (1,)), ((), ())),
                                 preferred_element_type=F32)
            y_off = _dot(cg, h16[:, g * gs:(g + 1) * gs]) * ecum_w[:, g * gs:(g + 1) * gs]
            diag = []
            for j in range(gm // LANES):
                m = (cb * decay[:, g * gm + j * LANES:g * gm + (j + 1) * LANES]).astype(BF16)
                c0 = g * gs + j * hb * SSD_HEAD_DIM
                xj = xd16[:, c0:c0 + hb * SSD_HEAD_DIM]
                diag.append(_dot(m, jnp.concatenate([xj] * hb, axis=0) * bdiag))
            y_parts.append(y_off + jnp.concatenate(diag, axis=-1))
            st_parts.append(lax.dot_general(bg, w_state[:, g * gs:(g + 1) * gs], (((0,), (0,)), ((), ())),
                                            preferred_element_type=F32))
        h_sc[...] = h_prev * ecum_w[cl - 1:cl, :] + jnp.concatenate(st_parts, axis=-1)
        y = jnp.concatenate(y_parts, axis=-1) + xs * dsk_ref[...]
        y = y * z_ref[pl.ds(r0, cl), :]
        normed = [_rms(y[:, g * gs:(g + 1) * gs], nw_ref[:, g * gs:(g + 1) * gs]) for g in range(SSD_GROUPS)]
        y_ref[pl.ds(r0, cl), :] = jnp.concatenate(normed, axis=-1).astype(y_ref.dtype)
        return carry

    lax.fori_loop(0, cps, chunk, 0, unroll=True)

    @pl.when(step == last)
    def _():
        hout_ref[...] = h_sc[...]


def _ssd_call(act, dt, z, h0_t, state_layer, w, batch, length, cps):
    cl = min(CHUNK, length)
    rows = cl * cps
    assert length % rows == 0 and LANES % cl == 0
    consts = _ssd_constants(cl)
    seq = lambda n: pl.BlockSpec((None, rows, n), lambda b, s: (b, s, 0))
    params = (w["a_log"], w["d_skip"], w["ssd_norm_w"])
    in_specs = ([seq(CONV_DIM), seq(LANES), seq(SSD_WIDTH),
                 pl.BlockSpec((None, None, SSD_STATE, SSD_WIDTH), lambda b, s: (state_layer, b, 0, 0))]
                + [pl.BlockSpec(a.shape, lambda b, s: (0, 0)) for a in params + consts])
    out_shape = (jax.ShapeDtypeStruct((batch, length, SSD_WIDTH), BF16),
                 jax.ShapeDtypeStruct((batch, SSD_STATE, SSD_WIDTH), F32))
    out_specs = (seq(SSD_WIDTH), pl.BlockSpec((None, SSD_STATE, SSD_WIDTH), lambda b, s: (b, 0, 0)))
    return pl.pallas_call(
        functools.partial(_ssd_kernel, cl=cl, cps=cps),
        grid=(batch, length // rows), in_specs=in_specs, out_specs=out_specs, out_shape=out_shape,
        scratch_shapes=[pltpu.VMEM((SSD_STATE, SSD_WIDTH), F32), pltpu.VMEM((3 * rows, SSD_WIDTH), F32),
                        pltpu.VMEM((rows, SSD_HEADS * cl), F32)],
        compiler_params=pltpu.CompilerParams(dimension_semantics=("arbitrary", "arbitrary"),
                                             vmem_limit_bytes=VMEM_LIMIT),
        name="ssd",
    )(act.reshape(batch, length, CONV_DIM), dt.reshape(batch, length, LANES),
      z.reshape(batch, length, SSD_WIDTH), h0_t, *params, *consts)


def _post_kernel(mla_ref, ssd_ref, x_ref, wa_ref, wb_ref, fw_ref, o_ref, *, final):
    y = x_ref[...] + _dot(mla_ref[...], wa_ref[...]) + _dot(ssd_ref[...], wb_ref[...])
    o_ref[...] = _rms(y, fw_ref[...]) if final else y


def _post_call(mla, ssd, x, w, final_norm_w, final, tr):
    rows, d = x.shape
    row = lambda n: pl.BlockSpec((tr, n), lambda i: (i, 0))
    return pl.pallas_call(
        functools.partial(_post_kernel, final=final),
        grid=(rows // tr,),
        in_specs=[row(MLA_WIDTH), row(SSD_WIDTH), row(d), _const_spec(w["w_out_mla"].shape),
                  _const_spec(w["w_out_ssd"].shape), _const_spec(final_norm_w.shape)],
        out_specs=row(d), out_shape=jax.ShapeDtypeStruct((rows, d), F32),
        compiler_params=pltpu.CompilerParams(dimension_semantics=("arbitrary",), vmem_limit_bytes=VMEM_LIMIT),
        name="post",
    )(mla, ssd, x, w["w_out_mla"], w["w_out_ssd"], final_norm_w)


def _swap_halves(wcols):
    k, n = wcols.shape
    blocks = wcols.reshape(k, n // ROPE_DIM, 2, ROPE_DIM // 2)
    return blocks[:, :, ::-1, :].reshape(k, n)


def _layer_weights(i, norm_w, w_in, q_norm_w, w_uq, kv_norm_w, w_uk, w_uv, conv_w, conv_b, dt_bias, a_log, d_skip,
                   ssd_norm_w, w_out):
    d = w_in.shape[1]
    offs = np.cumsum((0, Q_LORA, KV_LORA, ROPE_DIM, MLA_WIDTH, SSD_WIDTH, CONV_DIM, SSD_HEADS))
    col = lambda j: w_in[i][:, offs[j]:offs[j + 1]]
    w_kpe = col(2)
    w_kpe_sw = _swap_halves(w_kpe)
    w_sm = jnp.concatenate([w_kpe, w_kpe, w_kpe_sw, w_kpe_sw], axis=1)
    w_qdt = jnp.concatenate([col(0), col(6), jnp.zeros((d, LANES - SSD_HEADS), F32)], axis=1)
    uq = w_uq[i].reshape(Q_LORA, MLA_HEADS, NOPE_DIM + ROPE_DIM)
    uq_pe = uq[:, :, NOPE_DIM:].reshape(Q_LORA, MLA_HEADS * ROPE_DIM)
    pad_heads = lambda v: jnp.concatenate([v, jnp.zeros((LANES - SSD_HEADS,), F32)])[None, :]
    return {
        "depth": w_in.shape[0],
        "norm_w": norm_w[i][None, :],
        "w_q": w_qdt.astype(BF16), "w_kv": col(1).astype(BF16), "w_g": col(3).astype(BF16),
        "w_z": col(4).astype(BF16), "w_xbc": col(5).astype(BF16), "w_sm": w_sm.astype(BF16),
        "q_norm_w": q_norm_w[i][None, :],
        "w_uq_nope": uq[:, :, :NOPE_DIM].reshape(Q_LORA, MLA_HEADS * NOPE_DIM).astype(BF16),
        "w_uq_pe": uq_pe.astype(BF16), "w_uq_pe_sw": _swap_halves(uq_pe).astype(BF16),
        "kv_norm_w": kv_norm_w[i][None, :],
        "w_uk_t": jnp.transpose(w_uk[i], (1, 2, 0)).astype(BF16),
        "w_uv_t": jnp.transpose(w_uv[i], (1, 0, 2)).astype(BF16),
        "w_uk_flat": w_uk[i].reshape(KV_LORA, MLA_HEADS * NOPE_DIM).astype(BF16),
        "w_uv_flat": w_uv[i].reshape(KV_LORA, MLA_HEADS * V_DIM).astype(BF16),
        "conv_w": conv_w[i], "conv_b": conv_b[i][None, :],
        "dt_bias": pad_heads(dt_bias[i]), "a_log": pad_heads(a_log[i]),
        "d_skip": jnp.repeat(d_skip[i], SSD_HEAD_DIM)[None, :],
        "ssd_norm_w": ssd_norm_w[i][None, :],
        "w_out_mla": w_out[i][:MLA_WIDTH].astype(BF16), "w_out_ssd": w_out[i][MLA_WIDTH:].astype(BF16),
    }


def _rope_tables(past, length, batch):
    half = ROPE_DIM // 2
    inv = 1.0 / (ROPE_THETA ** (np.arange(half, dtype=np.float64) / half))
    ang = (past + np.arange(length, dtype=np.float64))[:, None] * inv[None, :]
    cos, sin = np.cos(ang), np.sin(ang)
    reps = LANES // ROPE_DIM
    cos2 = np.tile(np.concatenate([cos, cos], axis=-1), (batch, reps))
    sin2 = np.tile(np.concatenate([-sin, sin], axis=-1), (batch, reps))
    return jnp.asarray(cos2, F32), jnp.asarray(sin2, F32)


def _state_to_lanes(h):
    b = h.shape[0]
    return jnp.transpose(h, (0, 3, 1, 2)).reshape(b, SSD_STATE, SSD_WIDTH)


def _state_from_lanes(ht):
    b = ht.shape[0]
    return jnp.transpose(ht.reshape(b, SSD_STATE, SSD_HEADS, SSD_HEAD_DIM), (0, 2, 3, 1))


def _pick(n, prefs):
    for p in prefs:
        if n % p == 0:
            return p
    return n


def _layer(x, batch, length, layer, state_layer, caches, conv_state, ssm_state_t, stacked, w, tables, final_norm_w):
    rows = batch * length
    tr = _pick(rows, (512, 256, 128))
    absorbed = caches is not None
    *attn_in, ckv, kpe, g, z, act, dt, conv_new = _pre_call(x, conv_state, layer, state_layer, stacked, w, tables[0],
                                                            tables[1], batch, length, tr, absorbed)
    if absorbed:
        qcat, kcat = attn_in
        mla = _attn_sample_call(qcat, caches[0], caches[1], layer, kcat, g, w["w_uv_t"], batch, length,
                                _pick(caches[0].shape[2], (2048, 1024, 512, 256, 128, 64)))
    else:
        assert batch == 1
        tq = _pick(length, (2048, 1024, 512))
        mla = _attn_prompt_call(*attn_in, g, tq, min(tq // 4, 512))
    cl = min(CHUNK, length)
    ssd, h_t = _ssd_call(act, dt, z, ssm_state_t, state_layer, w, batch, length, _pick(length // cl, (8, 4, 2, 1)))
    y = _post_call(mla, ssd.reshape(rows, SSD_WIDTH), x, w, final_norm_w, layer == w["depth"] - 1,
                   _pick(rows, (1024, 512, 256, 128)))
    return y, (ckv, kpe), conv_new, h_t


def kernel(x_prompt, x_sample, cache_ckv, cache_kpe, state_conv, state_ssm, norm_w, w_in, q_norm_w, w_uq,
           kv_norm_w, w_uk, w_uv, conv_w, conv_b, dt_bias, a_log, d_skip, ssd_norm_w, w_out, final_norm_w):
    depth = w_in.shape[0]
    bp, lp, d = x_prompt.shape
    bs, ls, _ = x_sample.shape
    past = cache_ckv.shape[2]
    tab_p = _rope_tables(0, lp, bp)
    tab_s = _rope_tables(past, ls, bs)
    fw = final_norm_w[None, :]
    yp, ys = x_prompt.reshape(bp * lp, d), x_sample.reshape(bs * ls, d)
    zero_conv = jnp.zeros((1, bp, CONV_W - 1, CONV_DIM), F32)
    zero_ssm = jnp.zeros((1, bp, SSD_STATE, SSD_WIDTH), F32)
    ssm_t = _state_to_lanes(state_ssm.reshape((depth * bs,) + state_ssm.shape[2:]))
    ssm_t = ssm_t.reshape(depth, bs, SSD_STATE, SSD_WIDTH)
    kpe_t = jnp.swapaxes(cache_kpe, 2, 3)
    kv_p = (jnp.zeros((depth, bp * lp, KV_LORA), F32), jnp.zeros((depth, bp * lp, ROPE_DIM), F32))
    kv_s = (jnp.zeros((depth, bs * ls, KV_LORA), F32), jnp.zeros((depth, bs * ls, ROPE_DIM), F32))
    conv_p, conv_s, ssm_p, ssm_s = [], [], [], []
    for i in range(depth):
        w = _layer_weights(i, norm_w, w_in, q_norm_w, w_uq, kv_norm_w, w_uk, w_uv, conv_w, conv_b, dt_bias, a_log,
                           d_skip, ssd_norm_w, w_out)
        yp, kv_p, c_new, h_new = _layer(yp, bp, lp, i, 0, None, zero_conv, zero_ssm, kv_p, w, tab_p, fw)
        conv_p.append(c_new)
        ssm_p.append(_state_from_lanes(h_new))
        ys, kv_s, c_new, h_new = _layer(ys, bs, ls, i, i, (cache_ckv, kpe_t), state_conv, ssm_t, kv_s, w,
                                        tab_s, fw)
        conv_s.append(c_new)
        ssm_s.append(_state_from_lanes(h_new))
    return (yp.reshape(bp, lp, d), ys.reshape(bs, ls, d),
            kv_p[0].reshape(depth, bp, lp, KV_LORA), kv_p[1].reshape(depth, bp, lp, ROPE_DIM),
            jnp.stack(conv_p), jnp.stack(ssm_p),
            kv_s[0].reshape(depth, bs, ls, KV_LORA), kv_s[1].reshape(depth, bs, ls, ROPE_DIM),
            jnp.stack(conv_s), jnp.stack(ssm_s))
```
